```python
import math
import jax
import jax.numpy as jnp
from jax import lax
import numpy as np

D_MODEL = 2048
BATCH = 2
SEQ = 16384
DEPTH = 1

RMS_EPS = 1e-6

RW_HEADS = 16
RW_HEAD_DIM = 64
RW_WIDTH = RW_HEADS * RW_HEAD_DIM
RW_DECAY_LORA = 64
RW_ICL_LORA = 64
RW_GATE_LORA = 160
RW_GN_EPS = 64e-5
RW_IN_SIZES = (RW_WIDTH, RW_WIDTH, RW_WIDTH, RW_DECAY_LORA, RW_DECAY_LORA,
               RW_ICL_LORA, RW_ICL_LORA, RW_GATE_LORA)
RW_IN = 3 * RW_WIDTH + 2 * RW_DECAY_LORA + 2 * RW_ICL_LORA + RW_GATE_LORA

GDN_QK_HEADS = 4
GDN_V_HEADS = 8
GDN_HEAD_DIM = 128
GDN_QK_WIDTH = GDN_QK_HEADS * GDN_HEAD_DIM
GDN_V_WIDTH = GDN_V_HEADS * GDN_HEAD_DIM
GDN_CONV_CH = 2 * GDN_QK_WIDTH + GDN_V_WIDTH
GDN_CONV = 5
GDN_CHUNK = 64
GDN_NORM_EPS = 1e-6
GDN_IN = GDN_CONV_CH + GDN_V_WIDTH + 4 * GDN_V_HEADS

GATE_IN = 2 * D_MODEL
IN_WIDTH = RW_IN + GDN_IN + GATE_IN

FFN_HIDDEN = -(-8 * D_MODEL // 768) * 256

kernel_name = "hybrid_rwkv7_gdn_gated_merge_encoder"


def _split(t, sizes):
    offs = []
    acc = 0
    for s in sizes[:-1]:
        acc += s
        offs.append(acc)
    return jnp.split(t, offs, axis=-1)


def rms_norm(x, gain):
    xf = x.astype(jnp.float32)
    y = xf * lax.rsqrt(jnp.mean(xf * xf, axis=-1, keepdims=True) + RMS_EPS)
    return (y * gain.astype(jnp.float32)).astype(x.dtype)


def l2_normalize(t):
    return t * lax.rsqrt(jnp.sum(t * t, axis=-1, keepdims=True) + 1e-6)


def centred_shift_mix(p, mu):
    zero = jnp.zeros_like(p[:, :1])
    prev = jnp.concatenate([zero, p[:, :-1]], axis=1)
    nxt = jnp.concatenate([p[:, 1:], zero], axis=1)
    return p + mu * (0.5 * (prev + nxt) - p)


def depthwise_conv_centred(x, w):
    c = x.shape[-1]
    k = w.shape[0]
    return lax.conv_general_dilated(
        x, w[:, None, :], window_strides=(1,), padding=[(k // 2, k // 2)],
        dimension_numbers=("NWC", "WIO", "NWC"), feature_group_count=c)


def rwkv7_scan(r, w, k, v, kk, a, reverse):
    b, s, h, n = r.shape
    xs = tuple(jnp.swapaxes(t, 0, 1) for t in (r, w, k, v, kk, a))

    def step(state, inp):
        r_t, w_t, k_t, v_t, kk_t, a_t = inp
        sa = jnp.einsum("bhvk,bhk->bhv", state, kk_t)
        state = (state * w_t[:, :, None, :]
                 - sa[..., None] * (kk_t * a_t)[:, :, None, :]
                 + v_t[..., None] * k_t[:, :, None, :])
        return state, jnp.einsum("bhvk,bhk->bhv", state, r_t)

    s0 = jnp.zeros((b, h, n, n), jnp.float32)
    _, ys = lax.scan(step, s0, xs, reverse=reverse)
    return jnp.swapaxes(ys, 0, 1)


def rwkv7_branch(p, mu, w0_f, w2_f, w0_b, w2_b, a0_f, a2_f, a0_b, a2_b,
                 g2, k_k, k_a, r_k, gn_w, gn_b):
    dtype = p.dtype
    b, s, _ = p.shape
    p = centred_shift_mix(p.astype(jnp.float32), mu)
    r, k, v, wd_f, wd_b, ad_f, ad_b, gd = _split(p, RW_IN_SIZES)

    def heads(t):
        return t.reshape(b, s, RW_HEADS, RW_HEAD_DIM)

    kk = l2_normalize(heads(k * k_k))
    g = jax.nn.sigmoid(gd) @ g2

    def direction(wd, w0, w2, ad, a0, a2, reverse):
        w_log = -jax.nn.softplus(-(w0 + jnp.tanh(wd) @ w2)) - 0.5
        decay = jnp.exp(-jnp.exp(w_log))
        a = jax.nn.sigmoid(a0 + ad @ a2)
        k_dir = k * (1.0 + (a - 1.0) * k_a)
        y = rwkv7_scan(heads(r), heads(decay), heads(k_dir), heads(v), kk, heads(a), reverse)
        return y, k_dir

    y_f, k_f = direction(wd_f, w0_f, w2_f, ad_f, a0_f, a2_f, False)
    y_b, k_b = direction(wd_b, w0_b, w2_b, ad_b, a0_b, a2_b, True)
    y = y_f + y_b
    mean = jnp.mean(y, axis=-1, keepdims=True)
    var = jnp.mean(jnp.square(y - mean), axis=-1, keepdims=True)
    y = ((y - mean) * lax.rsqrt(var + RW_GN_EPS) * gn_w.reshape(RW_HEADS, RW_HEAD_DIM)
         + gn_b.reshape(RW_HEADS, RW_HEAD_DIM))
    bonus = jnp.sum(heads(r) * heads(0.5 * (k_f + k_b)) * r_k, axis=-1, keepdims=True) * heads(v)
    y = (y + bonus).reshape(b, s, RW_WIDTH) * g
    return y.astype(dtype)


def gated_delta_chunked(q, k, v, g, beta):
    b, s, h, dk = q.shape
    dv = v.shape[-1]
    c = GDN_CHUNK
    n = s // c

    def chunks(t):
        t = t.astype(jnp.float32).reshape((b, n, c, h) + t.shape[3:])
        return jnp.moveaxis(t, 3, 1)

    q, k, v, g, beta = (chunks(t) for t in (q, k, v, g, beta))
    g = jnp.cumsum(g, axis=-1)
    kb = k * beta[..., None]
    vb = v * beta[..., None]
    idx = jnp.arange(c)
    incl = idx[:, None] >= idx[None, :]
    strict = idx[:, None] > idx[None, :]
    decay = jnp.exp(jnp.where(incl, g[..., :, None] - g[..., None, :], -jnp.inf))
    a_mat = jnp.einsum("bhnid,bhnjd->bhnij", kb, k) * decay
    m = jnp.eye(c, dtype=jnp.float32) + jnp.where(strict, a_mat, 0.0)
    rhs = jnp.concatenate([vb, kb * jnp.exp(g)[..., None]], axis=-1)
    sol = lax.linalg.triangular_solve(m, rhs, left_side=True, lower=True, unit_diagonal=True)
    u, w = sol[..., :dv], sol[..., dv:]
    qk = jnp.einsum("bhnid,bhnjd->bhnij", q, k) * decay
    qg = q * jnp.exp(g)[..., None]
    kg = k * jnp.exp(g[..., -1:] - g)[..., None]
    g_last = jnp.exp(g[..., -1])
    xs = tuple(jnp.moveaxis(t, 2, 0) for t in (u, w, qk, qg, kg, g_last))

    def step(state, inp):
        u_c, w_c, qk_c, qg_c, kg_c, gl_c = inp
        v_new = u_c - jnp.einsum("bhcd,bhde->bhce", w_c, state)
        o_c = (jnp.einsum("bhcd,bhde->bhce", qg_c, state)
               + jnp.einsum("bhij,bhje->bhie", qk_c, v_new))
        state = state * gl_c[..., None, None] + jnp.einsum("bhcd,bhce->bhde", kg_c, v_new)
        return state, o_c

    s0 = jnp.zeros((b, h, dk, dv), jnp.float32)
    _, o = lax.scan(step, s0, xs)
    return jnp.transpose(o, (1, 0, 3, 2, 4)).reshape(b, s, h, dv)


def gdn_branch(p, conv_w, a_log_f, dt_bias_f, a_log_b, dt_bias_b, norm_w):
    dtype = p.dtype
    b, s, _ = p.shape
    vh = GDN_V_HEADS
    p = p.astype(jnp.float32)
    qkv, z, al_f, al_b, be_f, be_b = _split(p, (GDN_CONV_CH, GDN_V_WIDTH, vh, vh, vh, vh))
    qkv = jax.nn.silu(depthwise_conv_centred(qkv, conv_w.astype(jnp.float32)))
    q, k, v = _split(qkv, (GDN_QK_WIDTH, GDN_QK_WIDTH, GDN_V_WIDTH))
    rep = GDN_V_HEADS // GDN_QK_HEADS
    q = jnp.repeat(l2_normalize(q.reshape(b, s, GDN_QK_HEADS, GDN_HEAD_DIM)), rep, axis=2) * (GDN_HEAD_DIM ** -0.5)
    k = jnp.repeat(l2_normalize(k.reshape(b, s, GDN_QK_HEADS, GDN_HEAD_DIM)), rep, axis=2)
    v = v.reshape(b, s, vh, GDN_HEAD_DIM)
    g_f = -jnp.exp(a_log_f) * jax.nn.softplus(al_f + dt_bias_f)
    g_b = -jnp.exp(a_log_b) * jax.nn.softplus(al_b + dt_bias_b)
    o_f = gated_delta_chunked(q, k, v, g_f, jax.nn.sigmoid(be_f))

    def flip(t):
        return jnp.flip(t, axis=1)

    o_b = flip(gated_delta_chunked(flip(q), flip(k), flip(v), flip(g_b), flip(jax.nn.sigmoid(be_b))))
    o = o_f + o_b
    o = (o * lax.rsqrt(jnp.mean(o * o, axis=-1, keepdims=True) + GDN_NORM_EPS) * norm_w
         * jax.nn.silu(z.reshape(b, s, vh, GDN_HEAD_DIM)))
    return o.reshape(b, s, GDN_V_WIDTH).astype(dtype)


def setup_inputs(seed: int = 0) -> dict:
    key = jax.random.key(seed)
    keys = iter(jax.random.split(key, 40))
    f32 = jnp.float32
    L = DEPTH

    def normal(shape, scale):
        return scale * jax.random.normal(next(keys), shape, f32)

    def uniform(shape, lo, hi):
        return jax.random.uniform(next(keys), shape, f32, lo, hi)

    def gain(n):
        return 1.0 + normal((L, n), 0.02)

    def dt_bias():
        dt = jnp.exp(uniform((L, GDN_V_HEADS), math.log(1e-3), math.log(1e-1)))
        return dt + jnp.log(-jnp.expm1(-dt))

    return {
        "x": normal((BATCH, SEQ, D_MODEL), 1.0),
        "norm_pre_mix": gain(D_MODEL),
        "w_in": normal((L, D_MODEL, IN_WIDTH), D_MODEL ** -0.5),
        "rw_shift_mu": uniform((L, RW_IN), 0.0, 1.0),
        "rw_w0_f": uniform((L, RW_WIDTH), -6.5, -1.5),
        "rw_w2_f": normal((L, RW_DECAY_LORA, RW_WIDTH), 0.1 * RW_DECAY_LORA ** -0.5),
        "rw_w0_b": uniform((L, RW_WIDTH), -6.5, -1.5),
        "rw_w2_b": normal((L, RW_DECAY_LORA, RW_WIDTH), 0.1 * RW_DECAY_LORA ** -0.5),
        "rw_a0_f": normal((L, RW_WIDTH), 0.1),
        "rw_a2_f": normal((L, RW_ICL_LORA, RW_WIDTH), RW_ICL_LORA ** -0.5),
        "rw_a0_b": normal((L, RW_WIDTH), 0.1),
        "rw_a2_b": normal((L, RW_ICL_LORA, RW_WIDTH), RW_ICL_LORA ** -0.5),
        "rw_g2": normal((L, RW_GATE_LORA, RW_WIDTH), RW_GATE_LORA ** -0.5),
        "rw_k_k": 0.85 + normal((L, RW_WIDTH), 0.02),
        "rw_k_a": 1.0 + normal((L, RW_WIDTH), 0.02),
        "rw_r_k": normal((L, RW_HEADS, RW_HEAD_DIM), 0.1),
        "rw_gn_w": gain(RW_WIDTH),
        "rw_gn_b": normal((L, RW_WIDTH), 0.02),
        "gdn_conv_w": normal((L, GDN_CONV, GDN_CONV_CH), GDN_CONV ** -0.5),
        "gdn_a_log_f": jnp.log(uniform((L, GDN_V_HEADS), 1.0, 16.0)),
        "gdn_dt_bias_f": dt_bias(),
        "gdn_a_log_b": jnp.log(uniform((L, GDN_V_HEADS), 1.0, 16.0)),
        "gdn_dt_bias_b": dt_bias(),
        "gdn_norm_w": gain(GDN_HEAD_DIM),
        "w_branch_rw": normal((L, RW_WIDTH, D_MODEL), RW_WIDTH ** -0.5),
        "w_branch_gdn": normal((L, GDN_V_WIDTH, D_MODEL), GDN_V_WIDTH ** -0.5),
        "w_out": normal((L, D_MODEL, D_MODEL), D_MODEL ** -0.5),
        "norm_post_mix": gain(D_MODEL),
        "norm_pre_ffn": gain(D_MODEL),
        "w_ffn_gate": normal((L, D_MODEL, FFN_HIDDEN), D_MODEL ** -0.5),
        "w_ffn_up": normal((L, D_MODEL, FFN_HIDDEN), D_MODEL ** -0.5),
        "w_ffn_down": normal((L, FFN_HIDDEN, D_MODEL), FFN_HIDDEN ** -0.5),
        "norm_post_ffn": gain(D_MODEL),
    }


def reference(x, norm_pre_mix, w_in, rw_shift_mu, rw_w0_f, rw_w2_f, rw_w0_b, rw_w2_b,
              rw_a0_f, rw_a2_f, rw_a0_b, rw_a2_b, rw_g2, rw_k_k, rw_k_a, rw_r_k,
              rw_gn_w, rw_gn_b, gdn_conv_w, gdn_a_log_f, gdn_dt_bias_f, gdn_a_log_b,
              gdn_dt_bias_b, gdn_norm_w, w_branch_rw, w_branch_gdn, w_out,
              norm_post_mix, norm_pre_ffn, w_ffn_gate, w_ffn_up, w_ffn_down, norm_post_ffn):
    h = x
    for l in range(DEPTH):
        u = rms_norm(h, norm_pre_mix[l])
        p = jnp.einsum("bsd,de->bse", u, w_in[l])
        p_rw, p_gdn, p_gate = _split(p, (RW_IN, GDN_IN, GATE_IN))
        y_rw = rwkv7_branch(p_rw, rw_shift_mu[l], rw_w0_f[l], rw_w2_f[l], rw_w0_b[l], rw_w2_b[l],
                            rw_a0_f[l], rw_a2_f[l], rw_a0_b[l], rw_a2_b[l], rw_g2[l],
                            rw_k_k[l], rw_k_a[l], rw_r_k[l], rw_gn_w[l], rw_gn_b[l])
        y_gdn = gdn_branch(p_gdn, gdn_conv_w[l], gdn_a_log_f[l], gdn_dt_bias_f[l],
                           gdn_a_log_b[l], gdn_dt_bias_b[l], gdn_norm_w[l])
        gate_rw, gate_gdn = _split(p_gate, (D_MODEL, D_MODEL))
        merged = (jax.nn.sigmoid(gate_rw) * jnp.einsum("bse,ed->bsd", y_rw, w_branch_rw[l])
                  + jax.nn.sigmoid(gate_gdn) * jnp.einsum("bse,ed->bsd", y_gdn, w_branch_gdn[l]))
        h = h + rms_norm(jnp.einsum("bsd,de->bse", merged, w_out[l]), norm_post_mix[l])
        u = rms_norm(h, norm_pre_ffn[l])
        f = (jax.nn.silu(jnp.einsum("bsd,df->bsf", u, w_ffn_gate[l]))
             * jnp.einsum("bsd,df->bsf", u, w_ffn_up[l]))
        h = h + rms_norm(jnp.einsum("bsf,fd->bsd", f, w_ffn_down[l]), norm_post_ffn[l])
    return h
```

```python
import functools

import jax
import jax.numpy as jnp
from jax import lax
from jax.experimental import pallas as pl
from jax.experimental.pallas import tpu as pltpu

F32 = jnp.float32
BF16 = jnp.bfloat16

D_MODEL = 2048
RMS_EPS = 1e-6

RW_HEADS = 16
RW_HEAD_DIM = 64
RW_WIDTH = 1024
RW_LORA = 64
RW_GATE_LORA = 160
RW_GN_EPS = 64e-5
RW_IN = 3488

GDN_QK_HEADS = 4
GDN_V_HEADS = 8
GDN_HEAD_DIM = 128
GDN_QK_WIDTH = 512
GDN_V_WIDTH = 1024
GDN_CONV_CH = 2048
GDN_CONV = 5
GDN_NORM_EPS = 1e-6
GDN_IN = 3104
L2_EPS = 1e-6

FFN_HIDDEN = 5632

CHUNK = 64
LANES = 128
SUBLANES = 8

P_GQKV = 0
P_MG_RW = 2048
P_MG_GDN = 4096
P_GZ = 6144
P_RW = 7168
P_GG = 10752
P_WIDTH = 11264
RW_BLOCK = 3584

VMEM_LIMIT = 48 * 1024 * 1024


def _cparams(sem):
    return pltpu.CompilerParams(dimension_semantics=sem, vmem_limit_bytes=VMEM_LIMIT)


def _bf(x):
    return x.astype(BF16)


def _mm(a, b):
    return jnp.dot(_bf(a), _bf(b), preferred_element_type=F32)


def _mm_nt(a, b):
    return lax.dot_general(_bf(a), _bf(b), (((1,), (1,)), ((), ())), preferred_element_type=F32)


def _mm_tn(a, b):
    return lax.dot_general(_bf(a), _bf(b), (((0,), (0,)), ((), ())), preferred_element_type=F32)


def _split3(x):
    x1 = _bf(x)
    r1 = x - x1.astype(F32)
    x2 = _bf(r1)
    x3 = _bf(r1 - x2.astype(F32))
    return x1, x2, x3


def _mm01_left(m01, x):
    x1, x2, x3 = _split3(x)
    d = functools.partial(jnp.dot, preferred_element_type=F32)
    return d(m01, x1) + d(m01, x2) + d(m01, x3)


def _mm01_right(x, m01):
    x1, x2, x3 = _split3(x)
    d = functools.partial(jnp.dot, preferred_element_type=F32)
    return d(x1, m01) + d(x2, m01) + d(x3, m01)


def _seg_sum(x, ones_bd):
    n = x.shape[1] // LANES
    parts = [_mm01_right(x[:, j * LANES:(j + 1) * LANES], ones_bd) for j in range(n)]
    return parts[0] if n == 1 else jnp.concatenate(parts, axis=1)


def _sigmoid(x):
    return 1.0 / (1.0 + jnp.exp(-x))


def _softplus(x):
    return jnp.maximum(x, 0.0) + jnp.log1p(jnp.exp(-jnp.abs(x)))


def _stack_masked(x):
    lane = lax.broadcasted_iota(jnp.int32, x.shape, 1)
    lo = jnp.where(lane < CHUNK, x, 0.0)
    hi = jnp.where(lane >= CHUNK, x, 0.0)
    return jnp.concatenate([lo, hi], axis=0)


def _pair_masks(reverse):
    t = lax.broadcasted_iota(jnp.int32, (CHUNK, 2 * CHUNK), 0)
    s = lax.broadcasted_iota(jnp.int32, (CHUNK, 2 * CHUNK), 1) % CHUNK
    if reverse:
        return s > t, s >= t, s == t
    return s < t, s <= t, s == t


def _tri_incl(reverse):
    t = lax.broadcasted_iota(jnp.int32, (CHUNK, CHUNK), 0)
    s = lax.broadcasted_iota(jnp.int32, (CHUNK, CHUNK), 1)
    m = (s >= t) if reverse else (s <= t)
    return jnp.where(m, 1.0, 0.0).astype(BF16)


def _pair_inverse(a_pair, eye):
    n = -a_pair
    t = jnp.where(eye, 1.0, 0.0) + n
    steps = CHUNK.bit_length() - 2
    for _ in range(steps):
        n = _mm(n, _stack_masked(n))
        t = t + _mm(t, _stack_masked(n))
    return t


def _norm_matmul_kernel(x_ref, g_ref, w_ref, o_ref, u_ref):
    @pl.when(pl.program_id(1) == 0)
    def _():
        x = x_ref[...]
        ms = jnp.mean(x * x, axis=-1, keepdims=True)
        u_ref[...] = _bf(x * lax.rsqrt(ms + RMS_EPS) * g_ref[...])

    o_ref[...] = jnp.dot(u_ref[...], w_ref[...], preferred_element_type=F32)


def _norm_matmul(x, gain, w, tm, tn):
    t, d = x.shape
    n = w.shape[1]
    return pl.pallas_call(
        _norm_matmul_kernel,
        grid=(t // tm, n // tn),
        in_specs=[pl.BlockSpec((tm, d), lambda i, j: (i, 0)),
                  pl.BlockSpec((1, d), lambda i, j: (0, 0)),
                  pl.BlockSpec((d, tn), lambda i, j: (0, j))],
        out_specs=pl.BlockSpec((tm, tn), lambda i, j: (i, j)),
        out_shape=jax.ShapeDtypeStruct((t, n), F32),
        scratch_shapes=[pltpu.VMEM((tm, d), BF16)],
        compiler_params=_cparams(("parallel", "arbitrary")),
        name="in_proj",
    )(x, gain, w)


def _rw_prep_kernel(tiles_per_seq, p_ref, prev_ref, next_ref, mu_ref, w0_ref, w2_ref, a0_ref, a2_ref, g2_ref,
                    kk_ref, ka_ref, rk_ref, bd_ref,
                    r_o, v_o, kkn_o, g_o, bonus_o, lwf_o, kf_o, bf_o, lwb_o, kb_o, bb_o):
    i = pl.program_id(0)
    ts = p_ref.shape[0]
    pos = i % tiles_per_seq
    x = p_ref[...]
    prev_row = jnp.where(pos == 0, 0.0, prev_ref[SUBLANES - 1:SUBLANES, :])
    next_row = jnp.where(pos == tiles_per_seq - 1, 0.0, next_ref[0:1, :])
    row = lax.broadcasted_iota(jnp.int32, (ts, 1), 0)
    xp = jnp.where(row == 0, prev_row, pltpu.roll(x, 1, axis=0))
    xn = jnp.where(row == ts - 1, next_row, pltpu.roll(x, ts - 1, axis=0))
    pm = x + mu_ref[...] * (0.5 * (xp + xn) - x)

    w = RW_WIDTH
    r = pm[:, 0:w]
    k = pm[:, w:2 * w]
    v = pm[:, 2 * w:3 * w]
    lora = pm[:, 3 * w:]
    wlin = _mm(jnp.tanh(lora[:, 0:LANES]), w2_ref[...]) + w0_ref[...]
    lw = -jnp.exp(-_softplus(-wlin) - 0.5)
    a = _sigmoid(_mm(lora[:, LANES:2 * LANES], a2_ref[...]) + a0_ref[...])
    g = _mm(_sigmoid(lora[:, 2 * LANES:4 * LANES]), g2_ref[...])

    bd = bd_ref[...]
    kscaled = k * kk_ref[...]
    kkn = kscaled * lax.rsqrt(_seg_sum(kscaled * kscaled, bd) + L2_EPS)
    ka = ka_ref[...]
    a_f = a[:, 0:w]
    a_b = a[:, w:2 * w]
    k_f = k * (1.0 + (a_f - 1.0) * ka)
    k_b = k * (1.0 + (a_b - 1.0) * ka)
    bonus = _seg_sum(r * (0.5 * (k_f + k_b)) * rk_ref[...], bd) * v

    r_o[...] = r
    v_o[...] = v
    kkn_o[...] = kkn
    g_o[...] = g
    bonus_o[...] = bonus
    lwf_o[...] = lw[:, 0:w]
    kf_o[...] = k_f
    bf_o[...] = kkn * a_f
    lwb_o[...] = lw[:, w:2 * w]
    kb_o[...] = k_b
    bb_o[...] = kkn * a_b


def _rw_prep(p, seq, ts, mu, w0, w2, a0, a2, g2, k_k, k_a, r_k, bd64):
    t = p.shape[0]
    nb8 = t // SUBLANES
    rpb = ts // SUBLANES
    cb = P_RW // RW_BLOCK
    row = lambda width: pl.BlockSpec((1, width), lambda i: (0, 0))
    full = lambda a: pl.BlockSpec(a.shape, lambda i: (0, 0))
    out = jax.ShapeDtypeStruct((t, RW_WIDTH), F32)
    return pl.pallas_call(
        functools.partial(_rw_prep_kernel, seq // ts),
        grid=(t // ts,),
        in_specs=[pl.BlockSpec((ts, RW_BLOCK), lambda i: (i, cb)),
                  pl.BlockSpec((SUBLANES, RW_BLOCK), lambda i: (jnp.maximum(i * rpb - 1, 0), cb)),
                  pl.BlockSpec((SUBLANES, RW_BLOCK), lambda i: (jnp.minimum((i + 1) * rpb, nb8 - 1), cb)),
                  row(RW_BLOCK), row(2 * RW_WIDTH), full(w2), row(2 * RW_WIDTH), full(a2), full(g2),
                  row(RW_WIDTH), row(RW_WIDTH), row(RW_WIDTH), full(bd64)],
        out_specs=[pl.BlockSpec((ts, RW_WIDTH), lambda i: (i, 0))] * 11,
        out_shape=[out] * 11,
        compiler_params=_cparams(("parallel",)),
        name="rw_prep",
    )(p, p, p, mu, w0, w2, a0, a2, g2, k_k, k_a, r_k, bd64)


def _rw_chunk(r, v, kkn, lw, k, b, s_bd, reverse, consts):
    tri, strict, incl, eye, bd_mask = consts
    cum = _mm01_left(tri, lw)
    g_in = jnp.exp(cum)
    g_inv = jnp.exp(-cum)
    g_ex = jnp.exp(cum - lw)
    tot = cum[0:1, :] if reverse else cum[CHUNK - 1:CHUNK, :]
    r_t = r * g_in
    k_t = k * g_inv
    b_t = b * g_inv
    kk_t = kkn * g_ex

    lhs = jnp.concatenate([kk_t, r_t], axis=0)
    a_k = _mm_nt(lhs, _stack_masked(k_t))
    a_b = _mm_nt(lhs, _stack_masked(b_t))
    a_kv = jnp.where(strict, a_k[:CHUNK], 0.0)
    a_rk = jnp.where(incl, a_k[CHUNK:], 0.0)
    a_kb = jnp.where(strict, a_b[:CHUNK], 0.0)
    a_rb = jnp.where(incl, a_b[CHUNK:], 0.0)
    t_inv = _pair_inverse(a_kb, eye)

    v_st = _stack_masked(v)
    w_mat = _mm(t_inv, _stack_masked(kk_t))
    u0 = _mm(t_inv, _stack_masked(_mm(a_kv, v_st)))
    y0 = _mm(a_rk, v_st)

    ws = _mm_nt(jnp.concatenate([w_mat, r_t], axis=0), s_bd)
    u = ws[:CHUNK] + u0
    y = ws[CHUNK:] + y0 - _mm(a_rb, _stack_masked(u))
    ds = _mm_tn(jnp.concatenate([v, -u], axis=0), jnp.concatenate([k_t, b_t], axis=0))
    s_new = (s_bd + jnp.where(bd_mask, ds, 0.0)) * jnp.exp(tot)
    return y, s_new


def _rw_scan_kernel(reverse, r_ref, v_ref, kkn_ref, lw_ref, k_ref, b_ref, y_ref, s_ref):
    @pl.when(pl.program_id(2) == 0)
    def _():
        s_ref[...] = jnp.zeros_like(s_ref)

    strict, incl, eye = _pair_masks(reverse)
    ri = lax.broadcasted_iota(jnp.int32, (LANES, LANES), 0) // CHUNK
    ci = lax.broadcasted_iota(jnp.int32, (LANES, LANES), 1) // CHUNK
    consts = (_tri_incl(reverse), strict, incl, eye, ri == ci)
    n = r_ref.shape[0] // CHUNK
    s_bd = s_ref[...]
    for c in (range(n - 1, -1, -1) if reverse else range(n)):
        sl = slice(c * CHUNK, (c + 1) * CHUNK)
        y, s_bd = _rw_chunk(r_ref[sl, :], v_ref[sl, :], kkn_ref[sl, :], lw_ref[sl, :], k_ref[sl, :], b_ref[sl, :],
                            s_bd, reverse, consts)
        y_ref[sl, :] = y
    s_ref[...] = s_bd


def _rw_scan(r, v, kkn, lw, k, b, batch, tt, reverse):
    t = r.shape[0]
    nt = t // batch // tt
    pairs = RW_WIDTH // LANES
    if reverse:
        imap = lambda bi, pr, c: (bi * nt + nt - 1 - c, pr)
    else:
        imap = lambda bi, pr, c: (bi * nt + c, pr)
    spec = pl.BlockSpec((tt, LANES), imap)
    return pl.pallas_call(
        functools.partial(_rw_scan_kernel, reverse),
        grid=(batch, pairs, nt),
        in_specs=[spec] * 6,
        out_specs=spec,
        out_shape=jax.ShapeDtypeStruct((t, RW_WIDTH), F32),
        scratch_shapes=[pltpu.VMEM((LANES, LANES), F32)],
        compiler_params=_cparams(("parallel", "parallel", "arbitrary")),
        name="rw_scan_bwd" if reverse else "rw_scan_fwd",
    )(r, v, kkn, lw, k, b)


def _rw_post_kernel(yf_ref, yb_ref, bonus_ref, g_ref, gw_ref, gb_ref, bd_ref, o_ref):
    bd = bd_ref[...]
    y = yf_ref[...] + yb_ref[...]
    inv_n = 1.0 / RW_HEAD_DIM
    mean = _seg_sum(y, bd) * inv_n
    yc = y - mean
    var = _seg_sum(yc * yc, bd) * inv_n
    yn = yc * lax.rsqrt(var + RW_GN_EPS) * gw_ref[...] + gb_ref[...]
    o_ref[...] = _bf((yn + bonus_ref[...]) * g_ref[...])


def _rw_post(y_f, y_b, bonus, g, gn_w, gn_b, bd64, ts):
    t = y_f.shape[0]
    big = pl.BlockSpec((ts, RW_WIDTH), lambda i: (i, 0))
    row = pl.BlockSpec((1, RW_WIDTH), lambda i: (0, 0))
    return pl.pallas_call(
        _rw_post_kernel,
        grid=(t // ts,),
        in_specs=[big, big, big, big, row, row, pl.BlockSpec(bd64.shape, lambda i: (0, 0))],
        out_specs=big,
        out_shape=jax.ShapeDtypeStruct((t, RW_WIDTH), BF16),
        compiler_params=_cparams(("parallel",)),
        name="rw_post",
    )(y_f, y_b, bonus, g, gn_w, gn_b, bd64)


def _gdn_prep_kernel(tiles_per_seq, x_ref, prev_ref, next_ref, gate_ref, cw_ref, alog_ref, dtb_ref, ones_ref,
                     q_o, k_o, v_o, gates_o):
    i = pl.program_id(0)
    ts = x_ref.shape[0]
    pos = i % tiles_per_seq
    prev = jnp.where(pos == 0, 0.0, prev_ref[...])
    nxt = jnp.where(pos == tiles_per_seq - 1, 0.0, next_ref[...])
    ext = jnp.concatenate([prev, x_ref[...], nxt], axis=0)
    n = ts + 2 * SUBLANES
    half = GDN_CONV // 2
    acc = None
    for j in range(GDN_CONV):
        shift = (half - j) % n
        xs = ext if shift == 0 else pltpu.roll(ext, shift, axis=0)
        term = cw_ref[j:j + 1, :] * xs[SUBLANES:SUBLANES + ts, :]
        acc = term if acc is None else acc + term
    y = acc * _sigmoid(acc)

    ones = ones_ref[...]
    q = y[:, 0:GDN_QK_WIDTH]
    k = y[:, GDN_QK_WIDTH:2 * GDN_QK_WIDTH]
    q_o[...] = q * lax.rsqrt(_seg_sum(q * q, ones) + L2_EPS) * (GDN_HEAD_DIM ** -0.5)
    k_o[...] = k * lax.rsqrt(_seg_sum(k * k, ones) + L2_EPS)
    v_o[...] = y[:, 2 * GDN_QK_WIDTH:]

    gx = gate_ref[...]
    lane = lax.broadcasted_iota(jnp.int32, gx.shape, 1)
    log_decay = -jnp.exp(alog_ref[...]) * _softplus(gx + dtb_ref[...])
    gates_o[...] = jnp.where(lane < 2 * GDN_V_HEADS, log_decay, _sigmoid(gx))


def _gdn_prep(p, seq, ts, conv_w, a_log, dt_bias, ones128):
    t = p.shape[0]
    nb8 = t // SUBLANES
    rpb = ts // SUBLANES
    full = lambda a: pl.BlockSpec(a.shape, lambda i: (0, 0))
    return pl.pallas_call(
        functools.partial(_gdn_prep_kernel, seq // ts),
        grid=(t // ts,),
        in_specs=[pl.BlockSpec((ts, GDN_CONV_CH), lambda i: (i, P_GQKV // GDN_CONV_CH)),
                  pl.BlockSpec((SUBLANES, GDN_CONV_CH), lambda i: (jnp.maximum(i * rpb - 1, 0), P_GQKV // GDN_CONV_CH)),
                  pl.BlockSpec((SUBLANES, GDN_CONV_CH), lambda i: (jnp.minimum((i + 1) * rpb, nb8 - 1), P_GQKV // GDN_CONV_CH)),
                  pl.BlockSpec((ts, LANES), lambda i: (i, P_GG // LANES)),
                  full(conv_w), full(a_log), full(dt_bias), full(ones128)],
        out_specs=[pl.BlockSpec((ts, GDN_QK_WIDTH), lambda i: (i, 0)),
                   pl.BlockSpec((ts, GDN_QK_WIDTH), lambda i: (i, 0)),
                   pl.BlockSpec((ts, GDN_V_WIDTH), lambda i: (i, 0)),
                   pl.BlockSpec((ts, LANES), lambda i: (i, 0))],
        out_shape=[jax.ShapeDtypeStruct((t, GDN_QK_WIDTH), F32),
                   jax.ShapeDtypeStruct((t, GDN_QK_WIDTH), F32),
                   jax.ShapeDtypeStruct((t, GDN_V_WIDTH), F32),
                   jax.ShapeDtypeStruct((t, LANES), F32)],
        compiler_params=_cparams(("parallel",)),
        name="gdn_prep",
    )(p, p, p, p, conv_w, a_log, dt_bias, ones128)


def _gdn_chunk(q, k, v0, v1, gcol, grow, s0, s1, reverse, consts):
    tri, tri_bd, strict, incl, eye = consts
    lane2 = lax.broadcasted_iota(jnp.int32, (CHUNK, 2 * CHUNK), 1)
    first = lane2 < CHUNK

    ccol = _mm01_left(tri, gcol)
    crow = _mm01_right(grow, tri_bd)
    c0 = ccol[:, 0:1]
    c1 = ccol[:, 1:2]
    b0 = gcol[:, 2:3]
    b1 = gcol[:, 3:4]
    diff = jnp.where(first, c0, c1) - crow[0:1, :]
    decay = jnp.where(incl, jnp.exp(jnp.minimum(diff, 0.0)), 0.0)
    beta_pair = jnp.where(first, b0, b1)

    kdup = jnp.concatenate([k, k], axis=0)
    kk_pair = _mm_nt(k, kdup)
    qk_pair = _mm_nt(q, kdup)
    a_pair = jnp.where(strict, beta_pair * kk_pair * decay, 0.0)
    qk_pair = qk_pair * decay
    t_inv = _pair_inverse(a_pair, eye)

    e0 = jnp.exp(c0)
    e1 = jnp.exp(c1)
    zero = jnp.zeros((CHUNK, LANES), F32)
    rhs = jnp.concatenate([jnp.concatenate([v0 * b0, k * (b0 * e0), zero, zero], axis=1),
                           jnp.concatenate([zero, zero, v1 * b1, k * (b1 * e1)], axis=1)], axis=0)
    sol = _mm(t_inv, rhs)
    u0, w0 = sol[:, 0:LANES], sol[:, LANES:2 * LANES]
    u1, w1 = sol[:, 2 * LANES:3 * LANES], sol[:, 3 * LANES:]

    last = 0 if reverse else CHUNK - 1
    t0 = ccol[last:last + 1, 0:1]
    t1 = ccol[last:last + 1, 1:2]

    ws0 = _mm(jnp.concatenate([w0, q * e0], axis=0), s0)
    ws1 = _mm(jnp.concatenate([w1, q * e1], axis=0), s1)
    vn0 = u0 - ws0[:CHUNK]
    vn1 = u1 - ws1[:CHUNK]
    vn_bd = jnp.concatenate([jnp.concatenate([vn0, zero], axis=1),
                             jnp.concatenate([zero, vn1], axis=1)], axis=0)
    o_intra = _mm(qk_pair, vn_bd)
    o0 = ws0[CHUNK:] + o_intra[:, :LANES]
    o1 = ws1[CHUNK:] + o_intra[:, LANES:]
    s0 = s0 * jnp.exp(t0) + _mm_tn(k * jnp.exp(t0 - c0), vn0)
    s1 = s1 * jnp.exp(t1) + _mm_tn(k * jnp.exp(t1 - c1), vn1)
    return o0, o1, s0, s1


def _gdn_scan_kernel(reverse, q_ref, k_ref, v_ref, gcol_ref, grow_ref, o_ref, s_ref):
    @pl.when(pl.program_id(2) == 0)
    def _():
        s_ref[...] = jnp.zeros_like(s_ref)

    strict, incl, eye = _pair_masks(reverse)
    tri = _tri_incl(reverse)
    r2 = lax.broadcasted_iota(jnp.int32, (LANES, LANES), 0)
    c2 = lax.broadcasted_iota(jnp.int32, (LANES, LANES), 1)
    same = (r2 // CHUNK) == (c2 // CHUNK)
    order = (r2 >= c2) if reverse else (r2 <= c2)
    tri_bd = jnp.where(same & order, 1.0, 0.0).astype(BF16)
    consts = (tri, tri_bd, strict, incl, eye)
    n = q_ref.shape[0] // CHUNK
    s0 = s_ref[0]
    s1 = s_ref[1]
    for c in (range(n - 1, -1, -1) if reverse else range(n)):
        sl = slice(c * CHUNK, (c + 1) * CHUNK)
        o0, o1, s0, s1 = _gdn_chunk(q_ref[sl, :], k_ref[sl, :], v_ref[sl, 0:LANES], v_ref[sl, LANES:],
                                    gcol_ref[0, sl, :], grow_ref[0, c], s0, s1, reverse, consts)
        o_ref[sl, 0:LANES] = o0
        o_ref[sl, LANES:] = o1
    s_ref[0] = s0
    s_ref[1] = s1


def _gdn_scan(q, k, v, gcol, grow, batch, tt, reverse):
    t = q.shape[0]
    nt = t // batch // tt
    cpt = tt // CHUNK
    if reverse:
        rowi = lambda bi, c: bi * nt + nt - 1 - c
    else:
        rowi = lambda bi, c: bi * nt + c
    return pl.pallas_call(
        functools.partial(_gdn_scan_kernel, reverse),
        grid=(batch, GDN_QK_HEADS, nt),
        in_specs=[pl.BlockSpec((tt, LANES), lambda bi, j, c: (rowi(bi, c), j)),
                  pl.BlockSpec((tt, LANES), lambda bi, j, c: (rowi(bi, c), j)),
                  pl.BlockSpec((tt, 2 * LANES), lambda bi, j, c: (rowi(bi, c), j)),
                  pl.BlockSpec((1, tt, LANES), lambda bi, j, c: (j, rowi(bi, c), 0)),
                  pl.BlockSpec((1, cpt, SUBLANES, LANES), lambda bi, j, c: (j, rowi(bi, c), 0, 0))],
        out_specs=pl.BlockSpec((tt, 2 * LANES), lambda bi, j, c: (rowi(bi, c), j)),
        out_shape=jax.ShapeDtypeStruct((t, GDN_V_WIDTH), F32),
        scratch_shapes=[pltpu.VMEM((2, LANES, LANES), F32)],
        compiler_params=_cparams(("parallel", "parallel", "arbitrary")),
        name="gdn_scan_bwd" if reverse else "gdn_scan_fwd",
    )(q, k, v, gcol, grow)


def _gdn_gate_layouts(gates, off_g, off_b):
    t = gates.shape[0]
    g = gates[:, off_g:off_g + GDN_V_HEADS].reshape(t, GDN_QK_HEADS, 2)
    b = gates[:, off_b:off_b + GDN_V_HEADS].reshape(t, GDN_QK_HEADS, 2)
    col = jnp.concatenate([g, b], axis=-1)
    col = jnp.pad(jnp.transpose(col, (1, 0, 2)), ((0, 0), (0, 0), (0, LANES - 4)))
    row = jnp.transpose(g.reshape(t // CHUNK, CHUNK, GDN_QK_HEADS, 2), (2, 0, 3, 1))
    row = row.reshape(GDN_QK_HEADS, t // CHUNK, 1, 2 * CHUNK)
    row = jnp.pad(row, ((0, 0), (0, 0), (0, SUBLANES - 1), (0, 0)))
    return col, row


def _gdn_post_kernel(of_ref, ob_ref, z_ref, nw_ref, ones_ref, y_ref):
    o = of_ref[...] + ob_ref[...]
    ms = _seg_sum(o * o, ones_ref[...]) * (1.0 / GDN_HEAD_DIM)
    z = z_ref[...]
    y_ref[...] = _bf(o * lax.rsqrt(ms + GDN_NORM_EPS) * nw_ref[...] * (z * _sigmoid(z)))


def _gdn_post(o_f, o_b, p, norm_w, ones128, ts):
    t = o_f.shape[0]
    big = pl.BlockSpec((ts, GDN_V_WIDTH), lambda i: (i, 0))
    return pl.pallas_call(
        _gdn_post_kernel,
        grid=(t // ts,),
        in_specs=[big, big, pl.BlockSpec((ts, GDN_V_WIDTH), lambda i: (i, P_GZ // GDN_V_WIDTH)),
                  pl.BlockSpec((1, GDN_V_WIDTH), lambda i: (0, 0)), pl.BlockSpec(ones128.shape, lambda i: (0, 0))],
        out_specs=big,
        out_shape=jax.ShapeDtypeStruct((t, GDN_V_WIDTH), BF16),
        compiler_params=_cparams(("parallel",)),
        name="gdn_post",
    )(o_f, o_b, p, norm_w, ones128)


def _merge_kernel(yr_ref, yg_ref, gr_ref, gg_ref, pa_ref, pb_ref, o_ref):
    a = jnp.dot(yr_ref[...], pa_ref[...], preferred_element_type=F32)
    b = jnp.dot(yg_ref[...], pb_ref[...], preferred_element_type=F32)
    o_ref[...] = _bf(_sigmoid(gr_ref[...]) * a + _sigmoid(gg_ref[...]) * b)


def _merge(y_rw, y_gdn, p, w_a, w_b, tm, tn):
    t = y_rw.shape[0]
    nj = D_MODEL // tn
    return pl.pallas_call(
        _merge_kernel,
        grid=(t // tm, nj),
        in_specs=[pl.BlockSpec((tm, RW_WIDTH), lambda i, j: (i, 0)),
                  pl.BlockSpec((tm, GDN_V_WIDTH), lambda i, j: (i, 0)),
                  pl.BlockSpec((tm, tn), lambda i, j: (i, P_MG_RW // tn + j)),
                  pl.BlockSpec((tm, tn), lambda i, j: (i, P_MG_GDN // tn + j)),
                  pl.BlockSpec((RW_WIDTH, tn), lambda i, j: (0, j)),
                  pl.BlockSpec((GDN_V_WIDTH, tn), lambda i, j: (0, j))],
        out_specs=pl.BlockSpec((tm, tn), lambda i, j: (i, j)),
        out_shape=jax.ShapeDtypeStruct((t, D_MODEL), BF16),
        compiler_params=_cparams(("parallel", "arbitrary")),
        name="merge",
    )(y_rw, y_gdn, p, p, w_a, w_b)


def _out_proj_kernel(m_ref, x_ref, w_ref, g_ref, o_ref):
    y = jnp.dot(m_ref[...], w_ref[...], preferred_element_type=F32)
    ms = jnp.mean(y * y, axis=-1, keepdims=True)
    o_ref[...] = x_ref[...] + y * lax.rsqrt(ms + RMS_EPS) * g_ref[...]


def _out_proj(m, x, w, gain, tm):
    t, d = x.shape
    return pl.pallas_call(
        _out_proj_kernel,
        grid=(t // tm,),
        in_specs=[pl.BlockSpec((tm, d), lambda i: (i, 0)),
                  pl.BlockSpec((tm, d), lambda i: (i, 0)),
                  pl.BlockSpec((d, d), lambda i: (0, 0)),
                  pl.BlockSpec((1, d), lambda i: (0, 0))],
        out_specs=pl.BlockSpec((tm, d), lambda i: (i, 0)),
        out_shape=jax.ShapeDtypeStruct((t, d), F32),
        compiler_params=_cparams(("parallel",)),
        name="out_proj",
    )(m, x, w, gain)


def _ffn_kernel(h_ref, gpre_ref, wg_ref, wu_ref, wd_ref, gpost_ref, o_ref, u_ref, acc_ref):
    j = pl.program_id(1)

    @pl.when(j == 0)
    def _():
        x = h_ref[...]
        ms = jnp.mean(x * x, axis=-1, keepdims=True)
        u_ref[...] = _bf(x * lax.rsqrt(ms + RMS_EPS) * gpre_ref[...])
        acc_ref[...] = jnp.zeros_like(acc_ref)

    u = u_ref[...]
    gate = jnp.dot(u, wg_ref[...], preferred_element_type=F32)
    up = jnp.dot(u, wu_ref[...], preferred_element_type=F32)
    f = _bf(gate * _sigmoid(gate) * up)
    acc_ref[...] += jnp.dot(f, wd_ref[...], preferred_element_type=F32)

    @pl.when(j == pl.num_programs(1) - 1)
    def _():
        y = acc_ref[...]
        ms = jnp.mean(y * y, axis=-1, keepdims=True)
        o_ref[...] = h_ref[...] + y * lax.rsqrt(ms + RMS_EPS) * gpost_ref[...]


def _ffn(h, gpre, wg, wu, wd, gpost, tm, tf):
    t, d = h.shape
    f = wg.shape[1]
    return pl.pallas_call(
        _ffn_kernel,
        grid=(t // tm, f // tf),
        in_specs=[pl.BlockSpec((tm, d), lambda i, j: (i, 0)),
                  pl.BlockSpec((1, d), lambda i, j: (0, 0)),
                  pl.BlockSpec((d, tf), lambda i, j: (0, j)),
                  pl.BlockSpec((d, tf), lambda i, j: (0, j)),
                  pl.BlockSpec((tf, d), lambda i, j: (j, 0)),
                  pl.BlockSpec((1, d), lambda i, j: (0, 0))],
        out_specs=pl.BlockSpec((tm, d), lambda i, j: (i, 0)),
        out_shape=jax.ShapeDtypeStruct((t, d), F32),
        scratch_shapes=[pltpu.VMEM((tm, d), BF16), pltpu.VMEM((tm, d), F32)],
        compiler_params=_cparams(("parallel", "arbitrary")),
        name="ffn",
    )(h, gpre, wg, wu, wd, gpost)


def _pack_w_in(w_in):
    d = w_in.shape[0]
    rw = w_in[:, :RW_IN]
    gdn = w_in[:, RW_IN:RW_IN + GDN_IN]
    gates = w_in[:, RW_IN + GDN_IN:]
    z = lambda n: jnp.zeros((d, n), w_in.dtype)
    cols = [gdn[:, :GDN_CONV_CH],
            gates,
            gdn[:, GDN_CONV_CH:GDN_CONV_CH + GDN_V_WIDTH],
            rw, z(RW_BLOCK - RW_IN),
            gdn[:, GDN_CONV_CH + GDN_V_WIDTH:], z(LANES - 4 * GDN_V_HEADS),
            z(P_WIDTH - P_GG - LANES)]
    return _bf(jnp.concatenate(cols, axis=1))


def _block_diag2(a, b):
    z = jnp.zeros_like(a)
    return jnp.concatenate([jnp.concatenate([a, z], axis=1), jnp.concatenate([z, b], axis=1)], axis=0)


def _group_ones(group):
    i = jnp.arange(LANES)
    return (i[:, None] // group == i[None, :] // group).astype(BF16)


def _forward(x, norm_pre_mix, w_in, rw_shift_mu, rw_w0_f, rw_w2_f, rw_w0_b, rw_w2_b, rw_a0_f, rw_a2_f, rw_a0_b,
             rw_a2_b, rw_g2, rw_k_k, rw_k_a, rw_r_k, rw_gn_w, rw_gn_b, gdn_conv_w, gdn_a_log_f, gdn_dt_bias_f,
             gdn_a_log_b, gdn_dt_bias_b, gdn_norm_w, w_branch_rw, w_branch_gdn, w_out, norm_post_mix, norm_pre_ffn,
             w_ffn_gate, w_ffn_up, w_ffn_down, norm_post_ffn, *, tiles):
    batch, seq, d = x.shape
    t = batch * seq
    xf = x.reshape(t, d)
    row = lambda a: a.reshape(1, -1).astype(F32)
    bd64 = _group_ones(RW_HEAD_DIM)
    ones128 = _group_ones(GDN_HEAD_DIM)

    p = _norm_matmul(xf, row(norm_pre_mix), _pack_w_in(w_in), tiles["in_tm"], tiles["in_tn"])

    mu = jnp.pad(row(rw_shift_mu), ((0, 0), (0, RW_BLOCK - RW_IN)))
    w0 = jnp.concatenate([row(rw_w0_f), row(rw_w0_b)], axis=1)
    a0 = jnp.concatenate([row(rw_a0_f), row(rw_a0_b)], axis=1)
    w2 = _bf(_block_diag2(rw_w2_f, rw_w2_b))
    a2 = _bf(_block_diag2(rw_a2_f, rw_a2_b))
    g2 = _bf(jnp.pad(rw_g2, ((0, 2 * LANES - RW_GATE_LORA), (0, 0))))
    (r, v, kkn, g, bonus, lw_f, k_f, b_f, lw_b, k_b, b_b) = _rw_prep(
        p, seq, tiles["prep_ts"], mu, w0, w2, a0, a2, g2, row(rw_k_k), row(rw_k_a), row(rw_r_k), bd64)
    y_f = _rw_scan(r, v, kkn, lw_f, k_f, b_f, batch, tiles["scan_tt"], False)
    y_b = _rw_scan(r, v, kkn, lw_b, k_b, b_b, batch, tiles["scan_tt"], True)
    y_rw = _rw_post(y_f, y_b, bonus, g, row(rw_gn_w), row(rw_gn_b), bd64, tiles["prep_ts"])

    conv_w = jnp.pad(gdn_conv_w.astype(F32), ((0, SUBLANES - GDN_CONV), (0, 0)))
    nv = GDN_V_HEADS
    a_log = jnp.pad(jnp.concatenate([gdn_a_log_f, gdn_a_log_b]).reshape(1, -1), ((0, 0), (0, LANES - 2 * nv)))
    dt_bias = jnp.pad(jnp.concatenate([gdn_dt_bias_f, gdn_dt_bias_b]).reshape(1, -1), ((0, 0), (0, LANES - 2 * nv)))
    q, k, vg, gates = _gdn_prep(p, seq, tiles["prep_ts"], conv_w, a_log.astype(F32), dt_bias.astype(F32), ones128)
    gcol_f, grow_f = _gdn_gate_layouts(gates, 0, 2 * nv)
    gcol_b, grow_b = _gdn_gate_layouts(gates, nv, 3 * nv)
    o_f = _gdn_scan(q, k, vg, gcol_f, grow_f, batch, tiles["scan_tt"], False)
    o_b = _gdn_scan(q, k, vg, gcol_b, grow_b, batch, tiles["scan_tt"], True)
    nw = jnp.tile(row(gdn_norm_w), (1, GDN_V_HEADS))
    y_gdn = _gdn_post(o_f, o_b, p, nw, ones128, tiles["prep_ts"])

    m = _merge(y_rw, y_gdn, p, _bf(w_branch_rw), _bf(w_branch_gdn), tiles["merge_tm"], tiles["merge_tn"])
    h1 = _out_proj(m, xf, _bf(w_out), row(norm_post_mix), tiles["out_tm"])
    h2 = _ffn(h1, row(norm_pre_ffn), _bf(w_ffn_gate), _bf(w_ffn_up), _bf(w_ffn_down), row(norm_post_ffn),
              tiles["ffn_tm"], tiles["ffn_tf"])
    return h2.reshape(batch, seq, d)


def _tiles(seq):
    pick = lambda want: min(want, seq)
    return dict(in_tm=pick(1024), in_tn=1024, prep_ts=pick(256), scan_tt=pick(256),
                merge_tm=pick(512), merge_tn=1024, out_tm=pick(512), ffn_tm=pick(512), ffn_tf=512)


def kernel(x, norm_pre_mix, w_in, rw_shift_mu, rw_w0_f, rw_w2_f, rw_w0_b, rw_w2_b, rw_a0_f, rw_a2_f, rw_a0_b, rw_a2_b, rw_g2, rw_k_k, rw_k_a, rw_r_k, rw_gn_w, rw_gn_b, gdn_conv_w, gdn_a_log_f, gdn_dt_bias_f, gdn_a_log_b, gdn_dt_bias_b, gdn_norm_w, w_branch_rw, w_branch_gdn, w_out, norm_post_mix, norm_pre_ffn, w_ffn_gate, w_ffn_up, w_ffn_down, norm_post_ffn):
    args = [a[0] for a in (norm_pre_mix, w_in, rw_shift_mu, rw_w0_f, rw_w2_f, rw_w0_b, rw_w2_b, rw_a0_f, rw_a2_f,
                           rw_a0_b, rw_a2_b, rw_g2, rw_k_k, rw_k_a, rw_r_k, rw_gn_w, rw_gn_b, gdn_conv_w,
                           gdn_a_log_f, gdn_dt_bias_f, gdn_a_log_b, gdn_dt_bias_b, gdn_norm_w, w_branch_rw,
                           w_branch_gdn, w_out, norm_post_mix, norm_pre_ffn, w_ffn_gate, w_ffn_up, w_ffn_down,
                           norm_post_ffn)]
    assert norm_pre_mix.shape[0] == 1, "one layer"
    return _forward(x, *args, tiles=_tiles(x.shape[1]))
```

```python
import functools

import jax
import jax.numpy as jnp
from jax import lax
from jax.experimental import pallas as pl
from jax.experimental.pallas import tpu as pltpu

F32 = jnp.float32
BF16 = jnp.bfloat16

D_MODEL = 2048
RMS_EPS = 1e-6

RW_HEADS = 16
RW_HEAD_DIM = 64
RW_WIDTH = 1024
RW_LORA = 64
RW_GATE_LORA = 160
RW_GN_EPS = 64e-5
RW_IN = 3488

GDN_QK_HEADS = 4
GDN_V_HEADS = 8
GDN_HEAD_DIM = 128
GDN_QK_WIDTH = 512
GDN_V_WIDTH = 1024
GDN_CONV_CH = 2048
GDN_CONV = 5
GDN_NORM_EPS = 1e-6
GDN_IN = 3104
L2_EPS = 1e-6

FFN_HIDDEN = 5632

CHUNK = 64
LANES = 128
SUBLANES = 8

P_GQKV = 0
P_MG_RW = 2048
P_MG_GDN = 4096
P_GZ = 6144
P_RW = 7168
P_GG = 10752
P_WIDTH = 11264
RW_BLOCK = 3584

VMEM_LIMIT = 48 * 1024 * 1024


def _cparams(sem):
    return pltpu.CompilerParams(dimension_semantics=sem, vmem_limit_bytes=VMEM_LIMIT)


def _bf(x):
    return x.astype(BF16)


def _mm(a, b):
    return jnp.dot(_bf(a), _bf(b), preferred_element_type=F32)


def _mm_nt(a, b):
    return lax.dot_general(_bf(a), _bf(b), (((1,), (1,)), ((), ())), preferred_element_type=F32)


def _mm_tn(a, b):
    return lax.dot_general(_bf(a), _bf(b), (((0,), (0,)), ((), ())), preferred_element_type=F32)


def _split3(x):
    x1 = _bf(x)
    r1 = x - x1.astype(F32)
    x2 = _bf(r1)
    x3 = _bf(r1 - x2.astype(F32))
    return x1, x2, x3


def _mm01_left(m01, x):
    x1, x2, x3 = _split3(x)
    d = functools.partial(jnp.dot, preferred_element_type=F32)
    return d(m01, x1) + d(m01, x2) + d(m01, x3)


def _mm01_right(x, m01):
    x1, x2, x3 = _split3(x)
    d = functools.partial(jnp.dot, preferred_element_type=F32)
    return d(x1, m01) + d(x2, m01) + d(x3, m01)


def _seg_sum(x, ones_bd):
    n = x.shape[1] // LANES
    parts = [_mm01_right(x[:, j * LANES:(j + 1) * LANES], ones_bd) for j in range(n)]
    return parts[0] if n == 1 else jnp.concatenate(parts, axis=1)


def _sigmoid(x):
    return 1.0 / (1.0 + jnp.exp(-x))


def _softplus(x):
    return jnp.maximum(x, 0.0) + jnp.log1p(jnp.exp(-jnp.abs(x)))


def _stack_masked(x):
    group = lax.broadcasted_iota(jnp.int32, x.shape, 1) // CHUNK
    return jnp.concatenate([jnp.where(group == h, x, 0.0) for h in range(x.shape[1] // CHUNK)], axis=0)


def _side_masks(reverse, n):
    t = lax.broadcasted_iota(jnp.int32, (CHUNK, n * CHUNK), 0)
    s = lax.broadcasted_iota(jnp.int32, (CHUNK, n * CHUNK), 1) % CHUNK
    if reverse:
        return s > t, s >= t, s == t
    return s < t, s <= t, s == t


def _pair_masks(reverse):
    return _side_masks(reverse, 2)


def _tri_incl(reverse):
    t = lax.broadcasted_iota(jnp.int32, (CHUNK, CHUNK), 0)
    s = lax.broadcasted_iota(jnp.int32, (CHUNK, CHUNK), 1)
    m = (s >= t) if reverse else (s <= t)
    return jnp.where(m, 1.0, 0.0).astype(BF16)


def _pair_inverse(a_pair, eye):
    n, t = _inverse_init(a_pair, eye)
    for _ in range(INVERSE_STEPS):
        n, t = _inverse_step(n, t)
    return t


INVERSE_STEPS = CHUNK.bit_length() - 2


def _inverse_init(a_side, eye):
    n = -a_side
    return n, jnp.where(eye, 1.0, 0.0) + n


def _inverse_step(n, t):
    n = _mm(n, _stack_masked(n))
    return n, t + _mm(t, _stack_masked(n))


def _norm_matmul_kernel(x_ref, g_ref, w_ref, o_ref, u_ref):
    @pl.when(pl.program_id(1) == 0)
    def _():
        x = x_ref[...]
        ms = jnp.mean(x * x, axis=-1, keepdims=True)
        u_ref[...] = _bf(x * lax.rsqrt(ms + RMS_EPS) * g_ref[...])

    o_ref[...] = jnp.dot(u_ref[...], w_ref[...], preferred_element_type=F32)


def _norm_matmul(x, gain, w, tm, tn):
    t, d = x.shape
    n = w.shape[1]
    return pl.pallas_call(
        _norm_matmul_kernel,
        grid=(t // tm, n // tn),
        in_specs=[pl.BlockSpec((tm, d), lambda i, j: (i, 0)),
                  pl.BlockSpec((1, d), lambda i, j: (0, 0)),
                  pl.BlockSpec((d, tn), lambda i, j: (0, j))],
        out_specs=pl.BlockSpec((tm, tn), lambda i, j: (i, j)),
        out_shape=jax.ShapeDtypeStruct((t, n), F32),
        scratch_shapes=[pltpu.VMEM((tm, d), BF16)],
        compiler_params=_cparams(("parallel", "arbitrary")),
        name="in_proj",
    )(x, gain, w)


def _rw_prep_kernel(tiles_per_seq, p_ref, prev_ref, next_ref, mu_ref, w0_ref, w2_ref, a0_ref, a2_ref, g2_ref,
                    kk_ref, ka_ref, rk_ref, bd_ref,
                    r_o, v_o, kkn_o, g_o, bonus_o, lwf_o, kf_o, bf_o, lwb_o, kb_o, bb_o):
    i = pl.program_id(0)
    ts = p_ref.shape[0]
    pos = i % tiles_per_seq
    x = p_ref[...]
    prev_row = jnp.where(pos == 0, 0.0, prev_ref[SUBLANES - 1:SUBLANES, :])
    next_row = jnp.where(pos == tiles_per_seq - 1, 0.0, next_ref[0:1, :])
    row = lax.broadcasted_iota(jnp.int32, (ts, 1), 0)
    xp = jnp.where(row == 0, prev_row, pltpu.roll(x, 1, axis=0))
    xn = jnp.where(row == ts - 1, next_row, pltpu.roll(x, ts - 1, axis=0))
    pm = x + mu_ref[...] * (0.5 * (xp + xn) - x)

    w = RW_WIDTH
    r = pm[:, 0:w]
    k = pm[:, w:2 * w]
    v = pm[:, 2 * w:3 * w]
    lora = pm[:, 3 * w:]
    wlin = _mm(jnp.tanh(lora[:, 0:LANES]), w2_ref[...]) + w0_ref[...]
    lw = -jnp.exp(-_softplus(-wlin) - 0.5)
    a = _sigmoid(_mm(lora[:, LANES:2 * LANES], a2_ref[...]) + a0_ref[...])
    g = _mm(_sigmoid(lora[:, 2 * LANES:4 * LANES]), g2_ref[...])

    bd = bd_ref[...]
    kscaled = k * kk_ref[...]
    kkn = kscaled * lax.rsqrt(_seg_sum(kscaled * kscaled, bd) + L2_EPS)
    ka = ka_ref[...]
    a_f = a[:, 0:w]
    a_b = a[:, w:2 * w]
    k_f = k * (1.0 + (a_f - 1.0) * ka)
    k_b = k * (1.0 + (a_b - 1.0) * ka)
    bonus = _seg_sum(r * (0.5 * (k_f + k_b)) * rk_ref[...], bd) * v

    r_o[...] = r
    v_o[...] = v
    kkn_o[...] = kkn
    g_o[...] = g
    bonus_o[...] = bonus
    lwf_o[...] = lw[:, 0:w]
    kf_o[...] = k_f
    bf_o[...] = kkn * a_f
    lwb_o[...] = lw[:, w:2 * w]
    kb_o[...] = k_b
    bb_o[...] = kkn * a_b


def _rw_prep(p, seq, ts, mu, w0, w2, a0, a2, g2, k_k, k_a, r_k, bd64):
    t = p.shape[0]
    nb8 = t // SUBLANES
    rpb = ts // SUBLANES
    cb = P_RW // RW_BLOCK
    row = lambda width: pl.BlockSpec((1, width), lambda i: (0, 0))
    full = lambda a: pl.BlockSpec(a.shape, lambda i: (0, 0))
    out = jax.ShapeDtypeStruct((t, RW_WIDTH), F32)
    return pl.pallas_call(
        functools.partial(_rw_prep_kernel, seq // ts),
        grid=(t // ts,),
        in_specs=[pl.BlockSpec((ts, RW_BLOCK), lambda i: (i, cb)),
                  pl.BlockSpec((SUBLANES, RW_BLOCK), lambda i: (jnp.maximum(i * rpb - 1, 0), cb)),
                  pl.BlockSpec((SUBLANES, RW_BLOCK), lambda i: (jnp.minimum((i + 1) * rpb, nb8 - 1), cb)),
                  row(RW_BLOCK), row(2 * RW_WIDTH), full(w2), row(2 * RW_WIDTH), full(a2), full(g2),
                  row(RW_WIDTH), row(RW_WIDTH), row(RW_WIDTH), full(bd64)],
        out_specs=[pl.BlockSpec((ts, RW_WIDTH), lambda i: (i, 0))] * 11,
        out_shape=[out] * 11,
        compiler_params=_cparams(("parallel",)),
        name="rw_prep",
    )(p, p, p, mu, w0, w2, a0, a2, g2, k_k, k_a, r_k, bd64)


RW_GROUP = 4
RW_GROUP_W = RW_GROUP * RW_HEAD_DIM


def _rw_scan_kernel(reverse, r_ref, v_ref, kkn_ref, lw_ref, k_ref, b_ref, y_ref, s_ref):
    @pl.when(pl.program_id(1) == 0)
    def _():
        s_ref[...] = jnp.zeros_like(s_ref)

    gw = RW_GROUP_W
    nblk = r_ref.shape[1] // gw
    nch = r_ref.shape[0] // CHUNK
    strict, incl, eye = _side_masks(reverse, RW_GROUP)
    tri = _tri_incl(reverse)
    ri = lax.broadcasted_iota(jnp.int32, (gw, gw), 0) // CHUNK
    ci = lax.broadcasted_iota(jnp.int32, (gw, gw), 1) // CHUNK
    bd_mask = ri == ci
    order = list(range(nch - 1, -1, -1) if reverse else range(nch))
    chains = [(blk, c) for c in order for blk in range(nblk)]

    def tile(ref, blk, c):
        return ref[c * CHUNK:(c + 1) * CHUNK, blk * gw:(blk + 1) * gw]

    lw = {ch: tile(lw_ref, *ch) for ch in chains}
    cum = {ch: _mm01_left(tri, lw[ch]) for ch in chains}
    tot, r_t, k_t, b_t, kk_t, vv = {}, {}, {}, {}, {}, {}
    for ch in chains:
        g_inv = jnp.exp(-cum[ch])
        tot[ch] = cum[ch][0:1, :] if reverse else cum[ch][CHUNK - 1:CHUNK, :]
        r_t[ch] = tile(r_ref, *ch) * jnp.exp(cum[ch])
        k_t[ch] = tile(k_ref, *ch) * g_inv
        b_t[ch] = tile(b_ref, *ch) * g_inv
        kk_t[ch] = tile(kkn_ref, *ch) * jnp.exp(cum[ch] - lw[ch])
        vv[ch] = tile(v_ref, *ch)

    a_kv, a_rk, a_rb, nn, tt = {}, {}, {}, {}, {}
    for ch in chains:
        lhs = jnp.concatenate([kk_t[ch], r_t[ch]], axis=0)
        a_k = _mm_nt(lhs, _stack_masked(k_t[ch]))
        a_b = _mm_nt(lhs, _stack_masked(b_t[ch]))
        a_kv[ch] = jnp.where(strict, a_k[:CHUNK], 0.0)
        a_rk[ch] = jnp.where(incl, a_k[CHUNK:], 0.0)
        a_rb[ch] = jnp.where(incl, a_b[CHUNK:], 0.0)
        nn[ch], tt[ch] = _inverse_init(jnp.where(strict, a_b[:CHUNK], 0.0), eye)
    for _ in range(INVERSE_STEPS):
        for ch in chains:
            nn[ch], tt[ch] = _inverse_step(nn[ch], tt[ch])

    w_mat, u0, y0 = {}, {}, {}
    for ch in chains:
        v_st = _stack_masked(vv[ch])
        w_mat[ch] = _mm(tt[ch], _stack_masked(kk_t[ch]))
        y0[ch] = _mm(a_rk[ch], v_st)
        u0[ch] = _mm(a_kv[ch], v_st)
    for ch in chains:
        u0[ch] = _mm(tt[ch], _stack_masked(u0[ch]))

    state = [s_ref[blk] for blk in range(nblk)]
    for c in order:
        blks = range(nblk)
        ws = [_mm_nt(jnp.concatenate([w_mat[(blk, c)], r_t[(blk, c)]], axis=0), state[blk]) for blk in blks]
        u = [ws[blk][:CHUNK] + u0[(blk, c)] for blk in blks]
        ds = [_mm_tn(jnp.concatenate([vv[(blk, c)], -u[blk]], axis=0),
                     jnp.concatenate([k_t[(blk, c)], b_t[(blk, c)]], axis=0)) for blk in blks]
        state = [(state[blk] + jnp.where(bd_mask, ds[blk], 0.0)) * jnp.exp(tot[(blk, c)]) for blk in blks]
        for blk in blks:
            y = ws[blk][CHUNK:] + y0[(blk, c)] - _mm(a_rb[(blk, c)], _stack_masked(u[blk]))
            y_ref[c * CHUNK:(c + 1) * CHUNK, blk * gw:(blk + 1) * gw] = y
    for blk in range(nblk):
        s_ref[blk] = state[blk]


def _rw_scan(r, v, kkn, lw, k, b, batch, tt, reverse):
    t = r.shape[0]
    nt = t // batch // tt
    if reverse:
        imap = lambda bi, c: (bi * nt + nt - 1 - c, 0)
    else:
        imap = lambda bi, c: (bi * nt + c, 0)
    spec = pl.BlockSpec((tt, RW_WIDTH), imap)
    return pl.pallas_call(
        functools.partial(_rw_scan_kernel, reverse),
        grid=(batch, nt),
        in_specs=[spec] * 6,
        out_specs=spec,
        out_shape=jax.ShapeDtypeStruct((t, RW_WIDTH), F32),
        scratch_shapes=[pltpu.VMEM((RW_WIDTH // RW_GROUP_W, RW_GROUP_W, RW_GROUP_W), F32)],
        compiler_params=_cparams(("parallel", "arbitrary")),
        name="rw_scan_bwd" if reverse else "rw_scan_fwd",
    )(r, v, kkn, lw, k, b)


def _rw_post_kernel(yf_ref, yb_ref, bonus_ref, g_ref, gw_ref, gb_ref, bd_ref, o_ref):
    bd = bd_ref[...]
    y = yf_ref[...] + yb_ref[...]
    inv_n = 1.0 / RW_HEAD_DIM
    mean = _seg_sum(y, bd) * inv_n
    yc = y - mean
    var = _seg_sum(yc * yc, bd) * inv_n
    yn = yc * lax.rsqrt(var + RW_GN_EPS) * gw_ref[...] + gb_ref[...]
    o_ref[...] = _bf((yn + bonus_ref[...]) * g_ref[...])


def _rw_post(y_f, y_b, bonus, g, gn_w, gn_b, bd64, ts):
    t = y_f.shape[0]
    big = pl.BlockSpec((ts, RW_WIDTH), lambda i: (i, 0))
    row = pl.BlockSpec((1, RW_WIDTH), lambda i: (0, 0))
    return pl.pallas_call(
        _rw_post_kernel,
        grid=(t // ts,),
        in_specs=[big, big, big, big, row, row, pl.BlockSpec(bd64.shape, lambda i: (0, 0))],
        out_specs=big,
        out_shape=jax.ShapeDtypeStruct((t, RW_WIDTH), BF16),
        compiler_params=_cparams(("parallel",)),
        name="rw_post",
    )(y_f, y_b, bonus, g, gn_w, gn_b, bd64)


def _gdn_prep_kernel(tiles_per_seq, x_ref, prev_ref, next_ref, gate_ref, cw_ref, alog_ref, dtb_ref, ones_ref,
                     q_o, k_o, v_o, gates_o):
    i = pl.program_id(0)
    ts = x_ref.shape[0]
    pos = i % tiles_per_seq
    prev = jnp.where(pos == 0, 0.0, prev_ref[...])
    nxt = jnp.where(pos == tiles_per_seq - 1, 0.0, next_ref[...])
    ext = jnp.concatenate([prev, x_ref[...], nxt], axis=0)
    n = ts + 2 * SUBLANES
    half = GDN_CONV // 2
    acc = None
    for j in range(GDN_CONV):
        shift = (half - j) % n
        xs = ext if shift == 0 else pltpu.roll(ext, shift, axis=0)
        term = cw_ref[j:j + 1, :] * xs[SUBLANES:SUBLANES + ts, :]
        acc = term if acc is None else acc + term
    y = acc * _sigmoid(acc)

    ones = ones_ref[...]
    q = y[:, 0:GDN_QK_WIDTH]
    k = y[:, GDN_QK_WIDTH:2 * GDN_QK_WIDTH]
    q_o[...] = q * lax.rsqrt(_seg_sum(q * q, ones) + L2_EPS) * (GDN_HEAD_DIM ** -0.5)
    k_o[...] = k * lax.rsqrt(_seg_sum(k * k, ones) + L2_EPS)
    v_o[...] = y[:, 2 * GDN_QK_WIDTH:]

    gx = gate_ref[...]
    lane = lax.broadcasted_iota(jnp.int32, gx.shape, 1)
    log_decay = -jnp.exp(alog_ref[...]) * _softplus(gx + dtb_ref[...])
    gates_o[...] = jnp.where(lane < 2 * GDN_V_HEADS, log_decay, _sigmoid(gx))


def _gdn_prep(p, seq, ts, conv_w, a_log, dt_bias, ones128):
    t = p.shape[0]
    nb8 = t // SUBLANES
    rpb = ts // SUBLANES
    full = lambda a: pl.BlockSpec(a.shape, lambda i: (0, 0))
    return pl.pallas_call(
        functools.partial(_gdn_prep_kernel, seq // ts),
        grid=(t // ts,),
        in_specs=[pl.BlockSpec((ts, GDN_CONV_CH), lambda i: (i, P_GQKV // GDN_CONV_CH)),
                  pl.BlockSpec((SUBLANES, GDN_CONV_CH), lambda i: (jnp.maximum(i * rpb - 1, 0), P_GQKV // GDN_CONV_CH)),
                  pl.BlockSpec((SUBLANES, GDN_CONV_CH), lambda i: (jnp.minimum((i + 1) * rpb, nb8 - 1), P_GQKV // GDN_CONV_CH)),
                  pl.BlockSpec((ts, LANES), lambda i: (i, P_GG // LANES)),
                  full(conv_w), full(a_log), full(dt_bias), full(ones128)],
        out_specs=[pl.BlockSpec((ts, GDN_QK_WIDTH), lambda i: (i, 0)),
                   pl.BlockSpec((ts, GDN_QK_WIDTH), lambda i: (i, 0)),
                   pl.BlockSpec((ts, GDN_V_WIDTH), lambda i: (i, 0)),
                   pl.BlockSpec((ts, LANES), lambda i: (i, 0))],
        out_shape=[jax.ShapeDtypeStruct((t, GDN_QK_WIDTH), F32),
                   jax.ShapeDtypeStruct((t, GDN_QK_WIDTH), F32),
                   jax.ShapeDtypeStruct((t, GDN_V_WIDTH), F32),
                   jax.ShapeDtypeStruct((t, LANES), F32)],
        compiler_params=_cparams(("parallel",)),
        name="gdn_prep",
    )(p, p, p, p, conv_w, a_log, dt_bias, ones128)


def _gdn_scan_kernel(reverse, q_ref, k_ref, v_ref, gcol_ref, grow_ref, o_ref, s_ref):
    @pl.when(pl.program_id(1) == 0)
    def _():
        s_ref[...] = jnp.zeros_like(s_ref)

    nch = q_ref.shape[0] // CHUNK
    strict, incl, eye = _side_masks(reverse, 2)
    tri = _tri_incl(reverse)
    r2 = lax.broadcasted_iota(jnp.int32, (LANES, LANES), 0)
    c2 = lax.broadcasted_iota(jnp.int32, (LANES, LANES), 1)
    same = (r2 // CHUNK) == (c2 // CHUNK)
    before = (r2 >= c2) if reverse else (r2 <= c2)
    tri_bd = jnp.where(same & before, 1.0, 0.0).astype(BF16)
    first = lax.broadcasted_iota(jnp.int32, (CHUNK, 2 * CHUNK), 1) < CHUNK
    zero = jnp.zeros((CHUNK, LANES), F32)
    last = 0 if reverse else CHUNK - 1
    order = list(range(nch - 1, -1, -1) if reverse else range(nch))
    chains = [(j, c) for c in order for j in range(GDN_QK_HEADS)]

    def rows(c):
        return slice(c * CHUNK, (c + 1) * CHUNK)

    q, k, gcol, ccol, crow = {}, {}, {}, {}, {}
    for ch in chains:
        j, c = ch
        q[ch] = q_ref[rows(c), j * LANES:(j + 1) * LANES]
        k[ch] = k_ref[rows(c), j * LANES:(j + 1) * LANES]
        gcol[ch] = gcol_ref[j, rows(c), :]
        ccol[ch] = _mm01_left(tri, gcol[ch])
        crow[ch] = _mm01_right(grow_ref[j, c], tri_bd)

    qk_pair, nn, tt = {}, {}, {}
    for ch in chains:
        diff = jnp.where(first, ccol[ch][:, 0:1], ccol[ch][:, 1:2]) - crow[ch][0:1, :]
        decay = jnp.where(incl, jnp.exp(jnp.minimum(diff, 0.0)), 0.0)
        beta_pair = jnp.where(first, gcol[ch][:, 2:3], gcol[ch][:, 3:4])
        kdup = jnp.concatenate([k[ch], k[ch]], axis=0)
        kk_pair = _mm_nt(k[ch], kdup)
        qk_pair[ch] = _mm_nt(q[ch], kdup) * decay
        nn[ch], tt[ch] = _inverse_init(jnp.where(strict, beta_pair * kk_pair * decay, 0.0), eye)
    for _ in range(INVERSE_STEPS):
        for ch in chains:
            nn[ch], tt[ch] = _inverse_step(nn[ch], tt[ch])

    sol, wq = {}, {}
    for ch in chains:
        j, c = ch
        parts = []
        for e in range(2):
            beta = gcol[ch][:, 2 + e:3 + e]
            eg = jnp.exp(ccol[ch][:, e:e + 1])
            v_e = v_ref[rows(c), (2 * j + e) * LANES:(2 * j + e + 1) * LANES]
            blocks = [zero] * 4
            blocks[2 * e] = v_e * beta
            blocks[2 * e + 1] = k[ch] * (beta * eg)
            parts.append(jnp.concatenate(blocks, axis=1))
            wq[(j, c, e)] = q[ch] * eg
        sol[ch] = _mm(tt[ch], jnp.concatenate(parts, axis=0))

    state = {(j, e): s_ref[2 * j + e] for j in range(GDN_QK_HEADS) for e in range(2)}
    for c in order:
        heads = [(j, e) for j in range(GDN_QK_HEADS) for e in range(2)]
        ws = {}
        for (j, e) in heads:
            w_e = sol[(j, c)][:, (2 * e + 1) * LANES:(2 * e + 2) * LANES]
            ws[(j, e)] = _mm(jnp.concatenate([w_e, wq[(j, c, e)]], axis=0), state[(j, e)])
        vn = {(j, e): sol[(j, c)][:, 2 * e * LANES:(2 * e + 1) * LANES] - ws[(j, e)][:CHUNK] for (j, e) in heads}
        for (j, e) in heads:
            cc = ccol[(j, c)]
            tot = cc[last:last + 1, e:e + 1]
            kg = k[(j, c)] * jnp.exp(tot - cc[:, e:e + 1])
            state[(j, e)] = state[(j, e)] * jnp.exp(tot) + _mm_tn(kg, vn[(j, e)])
        for j in range(GDN_QK_HEADS):
            vn_bd = jnp.concatenate([jnp.concatenate([vn[(j, 0)], zero], axis=1),
                                     jnp.concatenate([zero, vn[(j, 1)]], axis=1)], axis=0)
            o_intra = _mm(qk_pair[(j, c)], vn_bd)
            for e in range(2):
                o_ref[rows(c), (2 * j + e) * LANES:(2 * j + e + 1) * LANES] = (
                    ws[(j, e)][CHUNK:] + o_intra[:, e * LANES:(e + 1) * LANES])
    for (j, e), s in state.items():
        s_ref[2 * j + e] = s


def _gdn_scan(q, k, v, gcol, grow, batch, tt, reverse):
    t = q.shape[0]
    nt = t // batch // tt
    cpt = tt // CHUNK
    if reverse:
        rowi = lambda bi, c: bi * nt + nt - 1 - c
    else:
        rowi = lambda bi, c: bi * nt + c
    nq = GDN_QK_HEADS
    return pl.pallas_call(
        functools.partial(_gdn_scan_kernel, reverse),
        grid=(batch, nt),
        in_specs=[pl.BlockSpec((tt, GDN_QK_WIDTH), lambda bi, c: (rowi(bi, c), 0)),
                  pl.BlockSpec((tt, GDN_QK_WIDTH), lambda bi, c: (rowi(bi, c), 0)),
                  pl.BlockSpec((tt, GDN_V_WIDTH), lambda bi, c: (rowi(bi, c), 0)),
                  pl.BlockSpec((nq, tt, LANES), lambda bi, c: (0, rowi(bi, c), 0)),
                  pl.BlockSpec((nq, cpt, SUBLANES, LANES), lambda bi, c: (0, rowi(bi, c), 0, 0))],
        out_specs=pl.BlockSpec((tt, GDN_V_WIDTH), lambda bi, c: (rowi(bi, c), 0)),
        out_shape=jax.ShapeDtypeStruct((t, GDN_V_WIDTH), F32),
        scratch_shapes=[pltpu.VMEM((GDN_V_HEADS, LANES, LANES), F32)],
        compiler_params=_cparams(("parallel", "arbitrary")),
        name="gdn_scan_bwd" if reverse else "gdn_scan_fwd",
    )(q, k, v, gcol, grow)


def _gdn_gate_layouts(gates, off_g, off_b):
    t = gates.shape[0]
    g = gates[:, off_g:off_g + GDN_V_HEADS].reshape(t, GDN_QK_HEADS, 2)
    b = gates[:, off_b:off_b + GDN_V_HEADS].reshape(t, GDN_QK_HEADS, 2)
    col = jnp.concatenate([g, b], axis=-1)
    col = jnp.pad(jnp.transpose(col, (1, 0, 2)), ((0, 0), (0, 0), (0, LANES - 4)))
    row = jnp.transpose(g.reshape(t // CHUNK, CHUNK, GDN_QK_HEADS, 2), (2, 0, 3, 1))
    row = row.reshape(GDN_QK_HEADS, t // CHUNK, 1, 2 * CHUNK)
    row = jnp.pad(row, ((0, 0), (0, 0), (0, SUBLANES - 1), (0, 0)))
    return col, row


def _gdn_post_kernel(of_ref, ob_ref, z_ref, nw_ref, ones_ref, y_ref):
    o = of_ref[...] + ob_ref[...]
    ms = _seg_sum(o * o, ones_ref[...]) * (1.0 / GDN_HEAD_DIM)
    z = z_ref[...]
    y_ref[...] = _bf(o * lax.rsqrt(ms + GDN_NORM_EPS) * nw_ref[...] * (z * _sigmoid(z)))


def _gdn_post(o_f, o_b, p, norm_w, ones128, ts):
    t = o_f.shape[0]
    big = pl.BlockSpec((ts, GDN_V_WIDTH), lambda i: (i, 0))
    return pl.pallas_call(
        _gdn_post_kernel,
        grid=(t // ts,),
        in_specs=[big, big, pl.BlockSpec((ts, GDN_V_WIDTH), lambda i: (i, P_GZ // GDN_V_WIDTH)),
                  pl.BlockSpec((1, GDN_V_WIDTH), lambda i: (0, 0)), pl.BlockSpec(ones128.shape, lambda i: (0, 0))],
        out_specs=big,
        out_shape=jax.ShapeDtypeStruct((t, GDN_V_WIDTH), BF16),
        compiler_params=_cparams(("parallel",)),
        name="gdn_post",
    )(o_f, o_b, p, norm_w, ones128)


def _merge_kernel(yr_ref, yg_ref, gr_ref, gg_ref, pa_ref, pb_ref, o_ref):
    a = jnp.dot(yr_ref[...], pa_ref[...], preferred_element_type=F32)
    b = jnp.dot(yg_ref[...], pb_ref[...], preferred_element_type=F32)
    o_ref[...] = _bf(_sigmoid(gr_ref[...]) * a + _sigmoid(gg_ref[...]) * b)


def _merge(y_rw, y_gdn, p, w_a, w_b, tm, tn):
    t = y_rw.shape[0]
    nj = D_MODEL // tn
    return pl.pallas_call(
        _merge_kernel,
        grid=(t // tm, nj),
        in_specs=[pl.BlockSpec((tm, RW_WIDTH), lambda i, j: (i, 0)),
                  pl.BlockSpec((tm, GDN_V_WIDTH), lambda i, j: (i, 0)),
                  pl.BlockSpec((tm, tn), lambda i, j: (i, P_MG_RW // tn + j)),
                  pl.BlockSpec((tm, tn), lambda i, j: (i, P_MG_GDN // tn + j)),
                  pl.BlockSpec((RW_WIDTH, tn), lambda i, j: (0, j)),
                  pl.BlockSpec((GDN_V_WIDTH, tn), lambda i, j: (0, j))],
        out_specs=pl.BlockSpec((tm, tn), lambda i, j: (i, j)),
        out_shape=jax.ShapeDtypeStruct((t, D_MODEL), BF16),
        compiler_params=_cparams(("parallel", "arbitrary")),
        name="merge",
    )(y_rw, y_gdn, p, p, w_a, w_b)


def _out_proj_kernel(m_ref, x_ref, w_ref, g_ref, o_ref):
    y = jnp.dot(m_ref[...], w_ref[...], preferred_element_type=F32)
    ms = jnp.mean(y * y, axis=-1, keepdims=True)
    o_ref[...] = x_ref[...] + y * lax.rsqrt(ms + RMS_EPS) * g_ref[...]


def _out_proj(m, x, w, gain, tm):
    t, d = x.shape
    return pl.pallas_call(
        _out_proj_kernel,
        grid=(t // tm,),
        in_specs=[pl.BlockSpec((tm, d), lambda i: (i, 0)),
                  pl.BlockSpec((tm, d), lambda i: (i, 0)),
                  pl.BlockSpec((d, d), lambda i: (0, 0)),
                  pl.BlockSpec((1, d), lambda i: (0, 0))],
        out_specs=pl.BlockSpec((tm, d), lambda i: (i, 0)),
        out_shape=jax.ShapeDtypeStruct((t, d), F32),
        compiler_params=_cparams(("parallel",)),
        name="out_proj",
    )(m, x, w, gain)


def _ffn_kernel(h_ref, gpre_ref, wg_ref, wu_ref, wd_ref, gpost_ref, o_ref, u_ref, acc_ref):
    j = pl.program_id(1)

    @pl.when(j == 0)
    def _():
        x = h_ref[...]
        ms = jnp.mean(x * x, axis=-1, keepdims=True)
        u_ref[...] = _bf(x * lax.rsqrt(ms + RMS_EPS) * gpre_ref[...])
        acc_ref[...] = jnp.zeros_like(acc_ref)

    u = u_ref[...]
    gate = jnp.dot(u, wg_ref[...], preferred_element_type=F32)
    up = jnp.dot(u, wu_ref[...], preferred_element_type=F32)
    f = _bf(gate * _sigmoid(gate) * up)
    acc_ref[...] += jnp.dot(f, wd_ref[...], preferred_element_type=F32)

    @pl.when(j == pl.num_programs(1) - 1)
    def _():
        y = acc_ref[...]
        ms = jnp.mean(y * y, axis=-1, keepdims=True)
        o_ref[...] = h_ref[...] + y * lax.rsqrt(ms + RMS_EPS) * gpost_ref[...]


def _ffn(h, gpre, wg, wu, wd, gpost, tm, tf):
    t, d = h.shape
    f = wg.shape[1]
    return pl.pallas_call(
        _ffn_kernel,
        grid=(t // tm, f // tf),
        in_specs=[pl.BlockSpec((tm, d), lambda i, j: (i, 0)),
                  pl.BlockSpec((1, d), lambda i, j: (0, 0)),
                  pl.BlockSpec((d, tf), lambda i, j: (0, j)),
                  pl.BlockSpec((d, tf), lambda i, j: (0, j)),
                  pl.BlockSpec((tf, d), lambda i, j: (j, 0)),
                  pl.BlockSpec((1, d), lambda i, j: (0, 0))],
        out_specs=pl.BlockSpec((tm, d), lambda i, j: (i, 0)),
        out_shape=jax.ShapeDtypeStruct((t, d), F32),
        scratch_shapes=[pltpu.VMEM((tm, d), BF16), pltpu.VMEM((tm, d), F32)],
        compiler_params=_cparams(("parallel", "arbitrary")),
        name="ffn",
    )(h, gpre, wg, wu, wd, gpost)


def _pack_w_in(w_in):
    d = w_in.shape[0]
    rw = w_in[:, :RW_IN]
    gdn = w_in[:, RW_IN:RW_IN + GDN_IN]
    gates = w_in[:, RW_IN + GDN_IN:]
    z = lambda n: jnp.zeros((d, n), w_in.dtype)
    cols = [gdn[:, :GDN_CONV_CH],
            gates,
            gdn[:, GDN_CONV_CH:GDN_CONV_CH + GDN_V_WIDTH],
            rw, z(RW_BLOCK - RW_IN),
            gdn[:, GDN_CONV_CH + GDN_V_WIDTH:], z(LANES - 4 * GDN_V_HEADS),
            z(P_WIDTH - P_GG - LANES)]
    return _bf(jnp.concatenate(cols, axis=1))


def _block_diag2(a, b):
    z = jnp.zeros_like(a)
    return jnp.concatenate([jnp.concatenate([a, z], axis=1), jnp.concatenate([z, b], axis=1)], axis=0)


def _group_ones(group):
    i = jnp.arange(LANES)
    return (i[:, None] // group == i[None, :] // group).astype(BF16)


def _forward(x, norm_pre_mix, w_in, rw_shift_mu, rw_w0_f, rw_w2_f, rw_w0_b, rw_w2_b, rw_a0_f, rw_a2_f, rw_a0_b,
             rw_a2_b, rw_g2, rw_k_k, rw_k_a, rw_r_k, rw_gn_w, rw_gn_b, gdn_conv_w, gdn_a_log_f, gdn_dt_bias_f,
             gdn_a_log_b, gdn_dt_bias_b, gdn_norm_w, w_branch_rw, w_branch_gdn, w_out, norm_post_mix, norm_pre_ffn,
             w_ffn_gate, w_ffn_up, w_ffn_down, norm_post_ffn, *, tiles):
    batch, seq, d = x.shape
    t = batch * seq
    xf = x.reshape(t, d)
    row = lambda a: a.reshape(1, -1).astype(F32)
    bd64 = _group_ones(RW_HEAD_DIM)
    ones128 = _group_ones(GDN_HEAD_DIM)

    p = _norm_matmul(xf, row(norm_pre_mix), _pack_w_in(w_in), tiles["in_tm"], tiles["in_tn"])

    mu = jnp.pad(row(rw_shift_mu), ((0, 0), (0, RW_BLOCK - RW_IN)))
    w0 = jnp.concatenate([row(rw_w0_f), row(rw_w0_b)], axis=1)
    a0 = jnp.concatenate([row(rw_a0_f), row(rw_a0_b)], axis=1)
    w2 = _bf(_block_diag2(rw_w2_f, rw_w2_b))
    a2 = _bf(_block_diag2(rw_a2_f, rw_a2_b))
    g2 = _bf(jnp.pad(rw_g2, ((0, 2 * LANES - RW_GATE_LORA), (0, 0))))
    (r, v, kkn, g, bonus, lw_f, k_f, b_f, lw_b, k_b, b_b) = _rw_prep(
        p, seq, tiles["prep_ts"], mu, w0, w2, a0, a2, g2, row(rw_k_k), row(rw_k_a), row(rw_r_k), bd64)
    y_f = _rw_scan(r, v, kkn, lw_f, k_f, b_f, batch, tiles["rw_tt"], False)
    y_b = _rw_scan(r, v, kkn, lw_b, k_b, b_b, batch, tiles["rw_tt"], True)
    y_rw = _rw_post(y_f, y_b, bonus, g, row(rw_gn_w), row(rw_gn_b), bd64, tiles["prep_ts"])

    conv_w = jnp.pad(gdn_conv_w.astype(F32), ((0, SUBLANES - GDN_CONV), (0, 0)))
    nv = GDN_V_HEADS
    a_log = jnp.pad(jnp.concatenate([gdn_a_log_f, gdn_a_log_b]).reshape(1, -1), ((0, 0), (0, LANES - 2 * nv)))
    dt_bias = jnp.pad(jnp.concatenate([gdn_dt_bias_f, gdn_dt_bias_b]).reshape(1, -1), ((0, 0), (0, LANES - 2 * nv)))
    q, k, vg, gates = _gdn_prep(p, seq, tiles["prep_ts"], conv_w, a_log.astype(F32), dt_bias.astype(F32), ones128)
    gcol_f, grow_f = _gdn_gate_layouts(gates, 0, 2 * nv)
    gcol_b, grow_b = _gdn_gate_layouts(gates, nv, 3 * nv)
    o_f = _gdn_scan(q, k, vg, gcol_f, grow_f, batch, tiles["gdn_tt"], False)
    o_b = _gdn_scan(q, k, vg, gcol_b, grow_b, batch, tiles["gdn_tt"], True)
    nw = jnp.tile(row(gdn_norm_w), (1, GDN_V_HEADS))
    y_gdn = _gdn_post(o_f, o_b, p, nw, ones128, tiles["prep_ts"])

    m = _merge(y_rw, y_gdn, p, _bf(w_branch_rw), _bf(w_branch_gdn), tiles["merge_tm"], tiles["merge_tn"])
    h1 = _out_proj(m, xf, _bf(w_out), row(norm_post_mix), tiles["out_tm"])
    h2 = _ffn(h1, row(norm_pre_ffn), _bf(w_ffn_gate), _bf(w_ffn_up), _bf(w_ffn_down), row(norm_post_ffn),
              tiles["ffn_tm"], tiles["ffn_tf"])
    return h2.reshape(batch, seq, d)


def _tiles(seq):
    pick = lambda want: min(want, seq)
    return dict(in_tm=pick(1024), in_tn=1024, prep_ts=pick(256), rw_tt=pick(128), gdn_tt=pick(128),
                merge_tm=pick(512), merge_tn=1024, out_tm=pick(512), ffn_tm=pick(512), ffn_tf=512)


def kernel(x, norm_pre_mix, w_in, rw_shift_mu, rw_w0_f, rw_w2_f, rw_w0_b, rw_w2_b, rw_a0_f, rw_a2_f, rw_a0_b, rw_a2_b, rw_g2, rw_k_k, rw_k_a, rw_r_k, rw_gn_w, rw_gn_b, gdn_conv_w, gdn_a_log_f, gdn_dt_bias_f, gdn_a_log_b, gdn_dt_bias_b, gdn_norm_w, w_branch_rw, w_branch_gdn, w_out, norm_post_mix, norm_pre_ffn, w_ffn_gate, w_ffn_up, w_ffn_down, norm_post_ffn):
    args = [a[0] for a in (norm_pre_mix, w_in, rw_shift_mu, rw_w0_f, rw_w2_f, rw_w0_b, rw_w2_b, rw_a0_f, rw_a2_f,
                           rw_a0_b, rw_a2_b, rw_g2, rw_k_k, rw_k_a, rw_r_k, rw_gn_w, rw_gn_b, gdn_conv_w,
                           gdn_a_log_f, gdn_dt_bias_f, gdn_a_log_b, gdn_dt_bias_b, gdn_norm_w, w_branch_rw,
                           w_branch_gdn, w_out, norm_post_mix, norm_pre_ffn, w_ffn_gate, w_ffn_up, w_ffn_down,
                           norm_post_ffn)]
    assert norm_pre_mix.shape[0] == 1, "one layer"
    return _forward(x, *args, tiles=_tiles(x.shape[1]))
```

```python
import functools

import jax
import jax.numpy as jnp
from jax import lax
from jax.experimental import pallas as pl
from jax.experimental.pallas import tpu as pltpu

F32 = jnp.float32
BF16 = jnp.bfloat16

D_MODEL = 2048
RMS_EPS = 1e-6

RW_HEADS = 16
RW_HEAD_DIM = 64
RW_WIDTH = 1024
RW_LORA = 64
RW_GATE_LORA = 160
RW_GN_EPS = 64e-5
RW_IN = 3488
RW_DECAY_SCALE = 0.6065306597126334

GDN_QK_HEADS = 4
GDN_V_HEADS = 8
GDN_HEAD_DIM = 128
GDN_QK_WIDTH = 512
GDN_V_WIDTH = 1024
GDN_CONV_CH = 2048
GDN_CONV = 5
GDN_NORM_EPS = 1e-6
GDN_IN = 3104
L2_EPS = 1e-6

FFN_HIDDEN = 5632

CHUNK = 64
LANES = 128
SUBLANES = 8

P_GQKV = 0
P_MG_RW = 2048
P_MG_GDN = 4096
P_GZ = 6144
P_RW = 7168
P_GG = 10752
P_WIDTH = 11264
RW_BLOCK = 3584

VMEM_LIMIT = 48 * 1024 * 1024


def _cparams(sem):
    return pltpu.CompilerParams(dimension_semantics=sem, vmem_limit_bytes=VMEM_LIMIT)


def _bf(x):
    return x.astype(BF16)


def _mm(a, b):
    return jnp.dot(_bf(a), _bf(b), preferred_element_type=F32)


def _mm_nt(a, b):
    return lax.dot_general(_bf(a), _bf(b), (((1,), (1,)), ((), ())), preferred_element_type=F32)


def _mm_tn(a, b):
    return lax.dot_general(_bf(a), _bf(b), (((0,), (0,)), ((), ())), preferred_element_type=F32)


def _split3(x):
    x1 = _bf(x)
    r1 = x - x1.astype(F32)
    x2 = _bf(r1)
    x3 = _bf(r1 - x2.astype(F32))
    return x1, x2, x3


def _mm01_left(m01, x):
    x1, x2, x3 = _split3(x)
    d = functools.partial(jnp.dot, preferred_element_type=F32)
    return d(m01, x1) + d(m01, x2) + d(m01, x3)


def _mm01_right(x, m01):
    x1 = _bf(x)
    x2 = _bf(x - x1.astype(F32))
    d = functools.partial(jnp.dot, preferred_element_type=F32)
    return d(x1, m01) + d(x2, m01)


def _mm01_tn(x, m01):
    x1, x2, x3 = _split3(x)
    d = functools.partial(lax.dot_general, dimension_numbers=(((0,), (0,)), ((), ())), preferred_element_type=F32)
    return d(x1, m01) + d(x2, m01) + d(x3, m01)


def _seg_sum(x, ones_bd):
    n = x.shape[1] // LANES
    parts = [_mm01_right(x[:, j * LANES:(j + 1) * LANES], ones_bd) for j in range(n)]
    return parts[0] if n == 1 else jnp.concatenate(parts, axis=1)


def _sigmoid(x):
    return 1.0 / (1.0 + jnp.exp(-x))


def _softplus(x):
    return jnp.maximum(x, 0.0) + jnp.log(1.0 + jnp.exp(-jnp.abs(x)))


def _stack_masked(x):
    group = lax.broadcasted_iota(jnp.int32, x.shape, 1) // CHUNK
    return jnp.concatenate([jnp.where(group == h, x, 0.0) for h in range(x.shape[1] // CHUNK)], axis=0)


def _side_masks(reverse, n):
    t = lax.broadcasted_iota(jnp.int32, (CHUNK, n * CHUNK), 0)
    s = lax.broadcasted_iota(jnp.int32, (CHUNK, n * CHUNK), 1) % CHUNK
    if reverse:
        return s > t, s >= t, s == t
    return s < t, s <= t, s == t


def _tri_incl(reverse):
    t = lax.broadcasted_iota(jnp.int32, (CHUNK, CHUNK), 0)
    s = lax.broadcasted_iota(jnp.int32, (CHUNK, CHUNK), 1)
    m = (s >= t) if reverse else (s <= t)
    return jnp.where(m, 1.0, 0.0).astype(BF16)


INVERSE_DOUBLINGS = CHUNK.bit_length() - 3


def _inverse_init(a_side, eye):
    n = -a_side
    return _mm(n, _stack_masked(n)), jnp.where(eye, 1.0, 0.0) + n


def _inverse_double(p, t):
    pt = _mm(jnp.concatenate([p, t], axis=0), _stack_masked(p))
    return pt[:CHUNK], t + pt[CHUNK:]


def _inverse_last(p, t):
    return t + _mm(t, _stack_masked(p))


def _norm_matmul_kernel(x_ref, g_ref, w_ref, o_ref, u_ref):
    @pl.when(pl.program_id(1) == 0)
    def _():
        x = x_ref[...]
        ms = jnp.mean(x * x, axis=-1, keepdims=True)
        u_ref[...] = _bf(x * lax.rsqrt(ms + RMS_EPS) * g_ref[...])

    o_ref[...] = jnp.dot(u_ref[...], w_ref[...], preferred_element_type=F32)


def _norm_matmul(x, gain, w, tm, tn):
    t, d = x.shape
    n = w.shape[1]
    return pl.pallas_call(
        _norm_matmul_kernel,
        grid=(t // tm, n // tn),
        in_specs=[pl.BlockSpec((tm, d), lambda i, j: (i, 0)),
                  pl.BlockSpec((1, d), lambda i, j: (0, 0)),
                  pl.BlockSpec((d, tn), lambda i, j: (0, j))],
        out_specs=pl.BlockSpec((tm, tn), lambda i, j: (i, j)),
        out_shape=jax.ShapeDtypeStruct((t, n), F32),
        scratch_shapes=[pltpu.VMEM((tm, d), BF16)],
        compiler_params=_cparams(("parallel", "arbitrary")),
        name="in_proj",
    )(x, gain, w)


def _rw_prep_kernel(tiles_per_seq, p_ref, prev_ref, next_ref, mu_ref, w0_ref, w2_ref, a0_ref, a2_ref, g2_ref,
                    kk_ref, ka_ref, rk_ref, bd_ref,
                    v_o, g_o, bonus_o, rf_o, kf_o, bf_o, kkf_o, gcf_o, rb_o, kb_o, bb_o, kkb_o, gcb_o):
    i = pl.program_id(0)
    ts = p_ref.shape[0]
    pos = i % tiles_per_seq
    x = p_ref[...]
    prev_row = jnp.where(pos == 0, 0.0, prev_ref[SUBLANES - 1:SUBLANES, :])
    next_row = jnp.where(pos == tiles_per_seq - 1, 0.0, next_ref[0:1, :])
    row = lax.broadcasted_iota(jnp.int32, (ts, 1), 0)
    xp = jnp.where(row == 0, prev_row, pltpu.roll(x, 1, axis=0))
    xn = jnp.where(row == ts - 1, next_row, pltpu.roll(x, ts - 1, axis=0))
    mu = mu_ref[...]
    pm = x * (1.0 - mu) + (xp + xn) * (0.5 * mu)

    w = RW_WIDTH
    r = pm[:, 0:w]
    k = pm[:, w:2 * w]
    v = pm[:, 2 * w:3 * w]
    lora = pm[:, 3 * w:]
    wlin = _mm(jnp.tanh(lora[:, 0:LANES]), w2_ref[...]) + w0_ref[...]
    lw = -RW_DECAY_SCALE * _sigmoid(wlin)
    a = _sigmoid(_mm(lora[:, LANES:2 * LANES], a2_ref[...]) + a0_ref[...])
    g = _mm(_sigmoid(lora[:, 2 * LANES:4 * LANES]), g2_ref[...])

    bd = bd_ref[...]
    kscaled = k * kk_ref[...]
    kkn = kscaled * lax.rsqrt(_seg_sum(kscaled * kscaled, bd) + L2_EPS)
    ka = ka_ref[...]
    a_f = a[:, 0:w]
    a_b = a[:, w:2 * w]
    k_f = k * (1.0 + (a_f - 1.0) * ka)
    k_b = k * (1.0 + (a_b - 1.0) * ka)
    bonus = _seg_sum(r * (0.5 * (k_f + k_b)) * rk_ref[...], bd) * v

    v_o[...] = _bf(v)
    g_o[...] = g
    bonus_o[...] = bonus

    rt = lax.broadcasted_iota(jnp.int32, (ts, ts), 0)
    rs = lax.broadcasted_iota(jnp.int32, (ts, ts), 1)
    same_chunk = (rt // CHUNK) == (rs // CHUNK)
    for d, (k_d, a_d, r_o, k_o, b_o, kk_o, gc_o) in enumerate(((k_f, a_f, rf_o, kf_o, bf_o, kkf_o, gcf_o),
                                                                (k_b, a_b, rb_o, kb_o, bb_o, kkb_o, gcb_o))):
        lw_d = lw[:, d * w:(d + 1) * w]
        before = (rs <= rt) if d == 0 else (rs >= rt)
        cum = _mm01_left(jnp.where(same_chunk & before, 1.0, 0.0).astype(BF16), lw_d)
        g_inv = jnp.exp(-cum)
        r_o[...] = _bf(r * jnp.exp(cum))
        k_o[...] = _bf(k_d * g_inv)
        b_o[...] = _bf(kkn * a_d * g_inv)
        kk_o[...] = _bf(kkn * jnp.exp(cum - lw_d))
        for c in range(ts // CHUNK):
            last = c * CHUNK + (CHUNK - 1 if d == 0 else 0)
            gc_o[c] = jnp.exp(cum[last:last + 1, :])


def _rw_prep(p, seq, ts, mu, w0, w2, a0, a2, g2, k_k, k_a, r_k, bd64):
    t = p.shape[0]
    nb8 = t // SUBLANES
    rpb = ts // SUBLANES
    cb = P_RW // RW_BLOCK
    row = lambda width: pl.BlockSpec((1, width), lambda i: (0, 0))
    full = lambda a: pl.BlockSpec(a.shape, lambda i: (0, 0))
    cpt = ts // CHUNK
    big = pl.BlockSpec((ts, RW_WIDTH), lambda i: (i, 0))
    tot = pl.BlockSpec((cpt, 1, RW_WIDTH), lambda i: (i, 0, 0))
    wide = lambda dt: jax.ShapeDtypeStruct((t, RW_WIDTH), dt)
    tots = jax.ShapeDtypeStruct((t // CHUNK, 1, RW_WIDTH), F32)
    direction = [wide(BF16)] * 4 + [tots]
    return pl.pallas_call(
        functools.partial(_rw_prep_kernel, seq // ts),
        grid=(t // ts,),
        in_specs=[pl.BlockSpec((ts, RW_BLOCK), lambda i: (i, cb)),
                  pl.BlockSpec((SUBLANES, RW_BLOCK), lambda i: (jnp.maximum(i * rpb - 1, 0), cb)),
                  pl.BlockSpec((SUBLANES, RW_BLOCK), lambda i: (jnp.minimum((i + 1) * rpb, nb8 - 1), cb)),
                  row(RW_BLOCK), row(2 * RW_WIDTH), full(w2), row(2 * RW_WIDTH), full(a2), full(g2),
                  row(RW_WIDTH), row(RW_WIDTH), row(RW_WIDTH), full(bd64)],
        out_specs=[big] * 3 + ([big] * 4 + [tot]) * 2,
        out_shape=[wide(BF16), wide(F32), wide(F32)] + direction * 2,
        compiler_params=_cparams(("parallel",)),
        name="rw_prep",
    )(p, p, p, mu, w0, w2, a0, a2, g2, k_k, k_a, r_k, bd64)


RW_GROUP = 4
RW_GROUP_W = RW_GROUP * RW_HEAD_DIM


def _rw_scan_kernel(reverse, v_ref, r_ref, k_ref, b_ref, kk_ref, gc_ref, y_ref, s_ref):
    @pl.when(pl.program_id(0) == 0)
    def _():
        s_ref[...] = jnp.zeros_like(s_ref)

    gw = RW_GROUP_W
    nb, rows, width = r_ref.shape
    nblk = width // gw
    nch = rows // CHUNK
    strict, incl, eye = _side_masks(reverse, RW_GROUP)
    ri = lax.broadcasted_iota(jnp.int32, (gw, gw), 0) // CHUNK
    ci = lax.broadcasted_iota(jnp.int32, (gw, gw), 1) // CHUNK
    bd_mask = ri == ci
    order = list(range(nch - 1, -1, -1) if reverse else range(nch))
    lanes = [(bi, blk) for bi in range(nb) for blk in range(nblk)]
    chains = [(bi, blk, c) for c in order for (bi, blk) in lanes]

    def tile(ref, bi, blk, c):
        return ref[bi, c * CHUNK:(c + 1) * CHUNK, blk * gw:(blk + 1) * gw]

    r_t = {ch: tile(r_ref, *ch) for ch in chains}
    k_t = {ch: tile(k_ref, *ch) for ch in chains}
    b_t = {ch: tile(b_ref, *ch) for ch in chains}
    kk_t = {ch: tile(kk_ref, *ch) for ch in chains}
    vv = {ch: tile(v_ref, *ch) for ch in chains}

    a_v, a_kb, a_rb, pp, tt = {}, {}, {}, {}, {}
    for ch in chains:
        lhs = jnp.concatenate([kk_t[ch], r_t[ch]], axis=0)
        a_k = _mm_nt(lhs, _stack_masked(k_t[ch]))
        a_b = _mm_nt(lhs, _stack_masked(b_t[ch]))
        a_v[ch] = jnp.concatenate([jnp.where(strict, a_k[:CHUNK], 0.0), jnp.where(incl, a_k[CHUNK:], 0.0)], axis=0)
        a_rb[ch] = jnp.where(incl, a_b[CHUNK:], 0.0)
        a_kb[ch] = jnp.where(strict, a_b[:CHUNK], 0.0)
    for ch in chains:
        pp[ch], tt[ch] = _inverse_init(a_kb[ch], eye)
    for _ in range(INVERSE_DOUBLINGS):
        for ch in chains:
            pp[ch], tt[ch] = _inverse_double(pp[ch], tt[ch])
    for ch in chains:
        tt[ch] = _inverse_last(pp[ch], tt[ch])

    w_mat, u0, y0 = {}, {}, {}
    for ch in chains:
        av = _mm(a_v[ch], _stack_masked(vv[ch]))
        y0[ch] = av[CHUNK:]
        u0[ch] = av[:CHUNK]
        w_mat[ch] = _mm(tt[ch], _stack_masked(kk_t[ch]))
    for ch in chains:
        u0[ch] = _mm(tt[ch], _stack_masked(u0[ch]))

    state = {(bi, blk): s_ref[bi * nblk + blk] for (bi, blk) in lanes}
    for c in order:
        ws = {ln: _mm_nt(jnp.concatenate([_bf(w_mat[(*ln, c)]), r_t[(*ln, c)]], axis=0), state[ln]) for ln in lanes}
        u = {ln: ws[ln][:CHUNK] + u0[(*ln, c)] for ln in lanes}
        ds = {ln: _mm_tn(jnp.concatenate([vv[(*ln, c)], _bf(-u[ln])], axis=0),
                         jnp.concatenate([k_t[(*ln, c)], b_t[(*ln, c)]], axis=0)) for ln in lanes}
        for (bi, blk) in lanes:
            ln = (bi, blk)
            state[ln] = (state[ln] + jnp.where(bd_mask, ds[ln], 0.0)) * gc_ref[bi, c, :, blk * gw:(blk + 1) * gw]
            y = ws[ln][CHUNK:] + y0[(bi, blk, c)] - _mm(a_rb[(bi, blk, c)], _stack_masked(u[ln]))
            y_ref[bi, c * CHUNK:(c + 1) * CHUNK, blk * gw:(blk + 1) * gw] = y
    for (bi, blk) in lanes:
        s_ref[bi * nblk + blk] = state[(bi, blk)]


def _rw_scan(v, r_t, k_t, b_t, kk_t, gc, batch, tt, reverse):
    t = v.shape[0]
    seq = t // batch
    nt = seq // tt
    cpt = tt // CHUNK
    rowi = (lambda c: nt - 1 - c) if reverse else (lambda c: c)
    spec = pl.BlockSpec((batch, tt, RW_WIDTH), lambda c: (0, rowi(c), 0))
    seqs = lambda a: a.reshape(batch, seq, RW_WIDTH)
    y = pl.pallas_call(
        functools.partial(_rw_scan_kernel, reverse),
        grid=(nt,),
        in_specs=[spec] * 5 + [pl.BlockSpec((batch, cpt, 1, RW_WIDTH), lambda c: (0, rowi(c), 0, 0))],
        out_specs=spec,
        out_shape=jax.ShapeDtypeStruct((batch, seq, RW_WIDTH), F32),
        scratch_shapes=[pltpu.VMEM((batch * RW_WIDTH // RW_GROUP_W, RW_GROUP_W, RW_GROUP_W), F32)],
        compiler_params=_cparams(("arbitrary",)),
        name="rw_scan_bwd" if reverse else "rw_scan_fwd",
    )(seqs(v), seqs(r_t), seqs(k_t), seqs(b_t), seqs(kk_t), gc.reshape(batch, seq // CHUNK, 1, RW_WIDTH))
    return y.reshape(t, RW_WIDTH)


def _rw_post_kernel(yf_ref, yb_ref, bonus_ref, g_ref, gw_ref, gb_ref, bd_ref, o_ref):
    bd = bd_ref[...]
    y = yf_ref[...] + yb_ref[...]
    inv_n = 1.0 / RW_HEAD_DIM
    mean = _seg_sum(y, bd) * inv_n
    yc = y - mean
    var = _seg_sum(yc * yc, bd) * inv_n
    yn = yc * lax.rsqrt(var + RW_GN_EPS) * gw_ref[...] + gb_ref[...]
    o_ref[...] = _bf((yn + bonus_ref[...]) * g_ref[...])


def _rw_post(y_f, y_b, bonus, g, gn_w, gn_b, bd64, ts):
    t = y_f.shape[0]
    big = pl.BlockSpec((ts, RW_WIDTH), lambda i: (i, 0))
    row = pl.BlockSpec((1, RW_WIDTH), lambda i: (0, 0))
    return pl.pallas_call(
        _rw_post_kernel,
        grid=(t // ts,),
        in_specs=[big, big, big, big, row, row, pl.BlockSpec(bd64.shape, lambda i: (0, 0))],
        out_specs=big,
        out_shape=jax.ShapeDtypeStruct((t, RW_WIDTH), BF16),
        compiler_params=_cparams(("parallel",)),
        name="rw_post",
    )(y_f, y_b, bonus, g, gn_w, gn_b, bd64)


def _gdn_prep_kernel(tiles_per_seq, x_ref, prev_ref, next_ref, gate_ref, cw_ref, alog_ref, dtb_ref, ones_ref,
                     q_o, k_o, v_o, gates_o):
    i = pl.program_id(0)
    ts = x_ref.shape[0]
    pos = i % tiles_per_seq
    prev = jnp.where(pos == 0, 0.0, prev_ref[...])
    nxt = jnp.where(pos == tiles_per_seq - 1, 0.0, next_ref[...])
    ext = jnp.concatenate([prev, x_ref[...], nxt], axis=0)
    n = ts + 2 * SUBLANES
    half = GDN_CONV // 2
    acc = None
    for j in range(GDN_CONV):
        shift = (half - j) % n
        xs = ext if shift == 0 else pltpu.roll(ext, shift, axis=0)
        term = cw_ref[j:j + 1, :] * xs[SUBLANES:SUBLANES + ts, :]
        acc = term if acc is None else acc + term
    y = acc * _sigmoid(acc)

    ones = ones_ref[...]
    q = y[:, 0:GDN_QK_WIDTH]
    k = y[:, GDN_QK_WIDTH:2 * GDN_QK_WIDTH]
    q_o[...] = q * lax.rsqrt(_seg_sum(q * q, ones) + L2_EPS) * (GDN_HEAD_DIM ** -0.5)
    k_o[...] = k * lax.rsqrt(_seg_sum(k * k, ones) + L2_EPS)
    v_o[...] = y[:, 2 * GDN_QK_WIDTH:]

    gx = gate_ref[...]
    lane = lax.broadcasted_iota(jnp.int32, gx.shape, 1)
    log_decay = -jnp.exp(alog_ref[...]) * _softplus(gx + dtb_ref[...])
    gates_o[...] = jnp.where(lane < 2 * GDN_V_HEADS, log_decay, _sigmoid(gx))


def _gdn_prep(p, seq, ts, conv_w, a_log, dt_bias, ones128):
    t = p.shape[0]
    nb8 = t // SUBLANES
    rpb = ts // SUBLANES
    full = lambda a: pl.BlockSpec(a.shape, lambda i: (0, 0))
    return pl.pallas_call(
        functools.partial(_gdn_prep_kernel, seq // ts),
        grid=(t // ts,),
        in_specs=[pl.BlockSpec((ts, GDN_CONV_CH), lambda i: (i, P_GQKV // GDN_CONV_CH)),
                  pl.BlockSpec((SUBLANES, GDN_CONV_CH), lambda i: (jnp.maximum(i * rpb - 1, 0), P_GQKV // GDN_CONV_CH)),
                  pl.BlockSpec((SUBLANES, GDN_CONV_CH), lambda i: (jnp.minimum((i + 1) * rpb, nb8 - 1), P_GQKV // GDN_CONV_CH)),
                  pl.BlockSpec((ts, LANES), lambda i: (i, P_GG // LANES)),
                  full(conv_w), full(a_log), full(dt_bias), full(ones128)],
        out_specs=[pl.BlockSpec((ts, GDN_QK_WIDTH), lambda i: (i, 0)),
                   pl.BlockSpec((ts, GDN_QK_WIDTH), lambda i: (i, 0)),
                   pl.BlockSpec((ts, GDN_V_WIDTH), lambda i: (i, 0)),
                   pl.BlockSpec((ts, LANES), lambda i: (i, 0))],
        out_shape=[jax.ShapeDtypeStruct((t, GDN_QK_WIDTH), F32),
                   jax.ShapeDtypeStruct((t, GDN_QK_WIDTH), F32),
                   jax.ShapeDtypeStruct((t, GDN_V_WIDTH), F32),
                   jax.ShapeDtypeStruct((t, LANES), F32)],
        compiler_params=_cparams(("parallel",)),
        name="gdn_prep",
    )(p, p, p, p, conv_w, a_log, dt_bias, ones128)


def _gdn_scan_kernel(reverse, g_off, b_off, q_ref, k_ref, v_ref, gates_ref, o_ref, s_ref):
    @pl.when(pl.program_id(0) == 0)
    def _():
        s_ref[...] = jnp.zeros_like(s_ref)

    nb = q_ref.shape[0]
    nch = q_ref.shape[1] // CHUNK
    strict, incl, _ = _side_masks(reverse, 2)
    eye4 = _side_masks(reverse, 4)[2]
    tri = _tri_incl(reverse)
    s_i = lax.broadcasted_iota(jnp.int32, (CHUNK, 2 * CHUNK), 0)
    t_i = lax.broadcasted_iota(jnp.int32, (CHUNK, 2 * CHUNK), 1) % CHUNK
    tri_t2 = jnp.where((s_i >= t_i) if reverse else (s_i <= t_i), 1.0, 0.0).astype(BF16)
    first = lax.broadcasted_iota(jnp.int32, (CHUNK, 2 * CHUNK), 1) < CHUNK
    zero = jnp.zeros((CHUNK, LANES), F32)
    zero_s = jnp.zeros((LANES, LANES), F32)
    last = 0 if reverse else CHUNK - 1
    order = list(range(nch - 1, -1, -1) if reverse else range(nch))
    slots = [(bi, c) for c in order for bi in range(nb)]
    chains = [(bi, j, c) for (bi, c) in slots for j in range(GDN_QK_HEADS)]

    def rows(c):
        return slice(c * CHUNK, (c + 1) * CHUNK)

    gates, ccol, crow = {}, {}, {}
    for sl in slots:
        gates[sl] = gates_ref[sl[0], rows(sl[1]), :]
        ccol[sl] = _mm01_left(tri, gates[sl])
        crow[sl] = _mm01_tn(gates[sl], tri_t2)

    q, k, qk_pair, a_pair = {}, {}, {}, {}
    for ch in chains:
        bi, j, c = ch
        sl = (bi, c)
        i0, i1 = g_off + 2 * j, g_off + 2 * j + 1
        q[ch] = q_ref[bi, rows(c), j * LANES:(j + 1) * LANES]
        k[ch] = k_ref[bi, rows(c), j * LANES:(j + 1) * LANES]
        diff = (jnp.where(first, ccol[sl][:, i0:i0 + 1], ccol[sl][:, i1:i1 + 1])
                - jnp.where(first, crow[sl][i0:i0 + 1, :], crow[sl][i1:i1 + 1, :]))
        decay = jnp.where(incl, jnp.exp(jnp.minimum(diff, 0.0)), 0.0)
        beta_pair = jnp.where(first, gates[sl][:, b_off + 2 * j:b_off + 2 * j + 1],
                              gates[sl][:, b_off + 2 * j + 1:b_off + 2 * j + 2])
        kdup = jnp.concatenate([k[ch], k[ch]], axis=0)
        kq = _mm_nt(jnp.concatenate([k[ch], q[ch]], axis=0), kdup)
        qk_pair[ch] = kq[CHUNK:] * decay
        a_pair[ch] = jnp.where(strict, beta_pair * kq[:CHUNK] * decay, 0.0)

    quads = [(bi, jj, c) for (bi, c) in slots for jj in range(GDN_QK_HEADS // 2)]
    pp, tq = {}, {}
    for (bi, jj, c) in quads:
        pp[(bi, jj, c)], tq[(bi, jj, c)] = _inverse_init(
            jnp.concatenate([a_pair[(bi, 2 * jj, c)], a_pair[(bi, 2 * jj + 1, c)]], axis=1), eye4)
    for _ in range(INVERSE_DOUBLINGS):
        for qd in quads:
            pp[qd], tq[qd] = _inverse_double(pp[qd], tq[qd])
    tt = {}
    for (bi, jj, c) in quads:
        t4 = _inverse_last(pp[(bi, jj, c)], tq[(bi, jj, c)])
        tt[(bi, 2 * jj, c)] = t4[:, :LANES]
        tt[(bi, 2 * jj + 1, c)] = t4[:, LANES:]

    sol, wq = {}, {}
    for ch in chains:
        bi, j, c = ch
        parts = []
        for e in range(2):
            h = 2 * j + e
            beta = gates[(bi, c)][:, b_off + h:b_off + h + 1]
            eg = jnp.exp(ccol[(bi, c)][:, g_off + h:g_off + h + 1])
            v_e = v_ref[bi, rows(c), h * LANES:(h + 1) * LANES]
            blocks = [zero] * 4
            blocks[2 * e] = v_e * beta
            blocks[2 * e + 1] = k[ch] * (beta * eg)
            parts.append(jnp.concatenate(blocks, axis=1))
            wq[(bi, j, c, e)] = q[ch] * eg
        sol[ch] = _mm(tt[ch], jnp.concatenate(parts, axis=0))

    pairs = [(bi, j) for bi in range(nb) for j in range(GDN_QK_HEADS)]
    heads = [(bi, j, e) for (bi, j) in pairs for e in range(2)]
    state = {(bi, j, e): s_ref[(bi * GDN_QK_HEADS + j) * 2 + e] for (bi, j, e) in heads}
    for c in order:
        ws = {}
        for (bi, j) in pairs:
            so = sol[(bi, j, c)]
            lhs = jnp.concatenate([jnp.concatenate([so[:, LANES:2 * LANES], so[:, 3 * LANES:]], axis=1),
                                   jnp.concatenate([wq[(bi, j, c, 0)], wq[(bi, j, c, 1)]], axis=1)], axis=0)
            s_bd = jnp.concatenate([jnp.concatenate([state[(bi, j, 0)], zero_s], axis=1),
                                    jnp.concatenate([zero_s, state[(bi, j, 1)]], axis=1)], axis=0)
            ws[(bi, j)] = _mm(lhs, s_bd)
        vn = {(bi, j, e): (sol[(bi, j, c)][:, 2 * e * LANES:(2 * e + 1) * LANES]
                           - ws[(bi, j)][:CHUNK, e * LANES:(e + 1) * LANES]) for (bi, j, e) in heads}
        for (bi, j, e) in heads:
            cc = ccol[(bi, c)][:, g_off + 2 * j + e:g_off + 2 * j + e + 1]
            tot = cc[last:last + 1, :]
            kg = k[(bi, j, c)] * jnp.exp(tot - cc)
            state[(bi, j, e)] = state[(bi, j, e)] * jnp.exp(tot) + _mm_tn(kg, vn[(bi, j, e)])
        for (bi, j) in pairs:
            vn_bd = jnp.concatenate([jnp.concatenate([vn[(bi, j, 0)], zero], axis=1),
                                     jnp.concatenate([zero, vn[(bi, j, 1)]], axis=1)], axis=0)
            o_intra = _mm(qk_pair[(bi, j, c)], vn_bd)
            o_ref[bi, rows(c), 2 * j * LANES:(2 * j + 2) * LANES] = ws[(bi, j)][CHUNK:] + o_intra
    for (bi, j, e) in heads:
        s_ref[(bi * GDN_QK_HEADS + j) * 2 + e] = state[(bi, j, e)]


def _gdn_scan(q, k, v, gates, batch, tt, reverse):
    t = q.shape[0]
    seq = t // batch
    nt = seq // tt
    rowi = (lambda c: nt - 1 - c) if reverse else (lambda c: c)
    g_off = GDN_V_HEADS if reverse else 0
    b_off = 2 * GDN_V_HEADS + g_off
    spec = lambda width: pl.BlockSpec((batch, tt, width), lambda c: (0, rowi(c), 0))
    seqs = lambda a: a.reshape(batch, seq, a.shape[-1])
    o = pl.pallas_call(
        functools.partial(_gdn_scan_kernel, reverse, g_off, b_off),
        grid=(nt,),
        in_specs=[spec(GDN_QK_WIDTH), spec(GDN_QK_WIDTH), spec(GDN_V_WIDTH), spec(LANES)],
        out_specs=spec(GDN_V_WIDTH),
        out_shape=jax.ShapeDtypeStruct((batch, seq, GDN_V_WIDTH), F32),
        scratch_shapes=[pltpu.VMEM((batch * GDN_V_HEADS, LANES, LANES), F32)],
        compiler_params=_cparams(("arbitrary",)),
        name="gdn_scan_bwd" if reverse else "gdn_scan_fwd",
    )(seqs(q), seqs(k), seqs(v), seqs(gates))
    return o.reshape(t, GDN_V_WIDTH)


def _gdn_post_kernel(of_ref, ob_ref, z_ref, nw_ref, ones_ref, y_ref):
    o = of_ref[...] + ob_ref[...]
    ms = _seg_sum(o * o, ones_ref[...]) * (1.0 / GDN_HEAD_DIM)
    z = z_ref[...]
    y_ref[...] = _bf(o * lax.rsqrt(ms + GDN_NORM_EPS) * nw_ref[...] * (z * _sigmoid(z)))


def _gdn_post(o_f, o_b, p, norm_w, ones128, ts):
    t = o_f.shape[0]
    big = pl.BlockSpec((ts, GDN_V_WIDTH), lambda i: (i, 0))
    return pl.pallas_call(
        _gdn_post_kernel,
        grid=(t // ts,),
        in_specs=[big, big, pl.BlockSpec((ts, GDN_V_WIDTH), lambda i: (i, P_GZ // GDN_V_WIDTH)),
                  pl.BlockSpec((1, GDN_V_WIDTH), lambda i: (0, 0)), pl.BlockSpec(ones128.shape, lambda i: (0, 0))],
        out_specs=big,
        out_shape=jax.ShapeDtypeStruct((t, GDN_V_WIDTH), BF16),
        compiler_params=_cparams(("parallel",)),
        name="gdn_post",
    )(o_f, o_b, p, norm_w, ones128)


def _merge_kernel(yr_ref, yg_ref, gr_ref, gg_ref, pa_ref, pb_ref, o_ref):
    a = jnp.dot(yr_ref[...], pa_ref[...], preferred_element_type=F32)
    b = jnp.dot(yg_ref[...], pb_ref[...], preferred_element_type=F32)
    o_ref[...] = _bf(_sigmoid(gr_ref[...]) * a + _sigmoid(gg_ref[...]) * b)


def _merge(y_rw, y_gdn, p, w_a, w_b, tm, tn):
    t = y_rw.shape[0]
    nj = D_MODEL // tn
    return pl.pallas_call(
        _merge_kernel,
        grid=(t // tm, nj),
        in_specs=[pl.BlockSpec((tm, RW_WIDTH), lambda i, j: (i, 0)),
                  pl.BlockSpec((tm, GDN_V_WIDTH), lambda i, j: (i, 0)),
                  pl.BlockSpec((tm, tn), lambda i, j: (i, P_MG_RW // tn + j)),
                  pl.BlockSpec((tm, tn), lambda i, j: (i, P_MG_GDN // tn + j)),
                  pl.BlockSpec((RW_WIDTH, tn), lambda i, j: (0, j)),
                  pl.BlockSpec((GDN_V_WIDTH, tn), lambda i, j: (0, j))],
        out_specs=pl.BlockSpec((tm, tn), lambda i, j: (i, j)),
        out_shape=jax.ShapeDtypeStruct((t, D_MODEL), BF16),
        compiler_params=_cparams(("parallel", "arbitrary")),
        name="merge",
    )(y_rw, y_gdn, p, p, w_a, w_b)


def _out_proj_kernel(m_ref, x_ref, w_ref, g_ref, o_ref):
    y = jnp.dot(m_ref[...], w_ref[...], preferred_element_type=F32)
    ms = jnp.mean(y * y, axis=-1, keepdims=True)
    o_ref[...] = x_ref[...] + y * lax.rsqrt(ms + RMS_EPS) * g_ref[...]


def _out_proj(m, x, w, gain, tm):
    t, d = x.shape
    return pl.pallas_call(
        _out_proj_kernel,
        grid=(t // tm,),
        in_specs=[pl.BlockSpec((tm, d), lambda i: (i, 0)),
                  pl.BlockSpec((tm, d), lambda i: (i, 0)),
                  pl.BlockSpec((d, d), lambda i: (0, 0)),
                  pl.BlockSpec((1, d), lambda i: (0, 0))],
        out_specs=pl.BlockSpec((tm, d), lambda i: (i, 0)),
        out_shape=jax.ShapeDtypeStruct((t, d), F32),
        compiler_params=_cparams(("parallel",)),
        name="out_proj",
    )(m, x, w, gain)


def _ffn_kernel(h_ref, gpre_ref, wg_ref, wu_ref, wd_ref, gpost_ref, o_ref, u_ref, acc_ref):
    j = pl.program_id(1)

    @pl.when(j == 0)
    def _():
        x = h_ref[...]
        ms = jnp.mean(x * x, axis=-1, keepdims=True)
        u_ref[...] = _bf(x * lax.rsqrt(ms + RMS_EPS) * gpre_ref[...])
        acc_ref[...] = jnp.zeros_like(acc_ref)

    u = u_ref[...]
    gate = jnp.dot(u, wg_ref[...], preferred_element_type=F32)
    up = jnp.dot(u, wu_ref[...], preferred_element_type=F32)
    f = _bf(gate * _sigmoid(gate) * up)
    acc_ref[...] += jnp.dot(f, wd_ref[...], preferred_element_type=F32)

    @pl.when(j == pl.num_programs(1) - 1)
    def _():
        y = acc_ref[...]
        ms = jnp.mean(y * y, axis=-1, keepdims=True)
        o_ref[...] = h_ref[...] + y * lax.rsqrt(ms + RMS_EPS) * gpost_ref[...]


def _ffn(h, gpre, wg, wu, wd, gpost, tm, tf):
    t, d = h.shape
    f = wg.shape[1]
    return pl.pallas_call(
        _ffn_kernel,
        grid=(t // tm, f // tf),
        in_specs=[pl.BlockSpec((tm, d), lambda i, j: (i, 0)),
                  pl.BlockSpec((1, d), lambda i, j: (0, 0)),
                  pl.BlockSpec((d, tf), lambda i, j: (0, j)),
                  pl.BlockSpec((d, tf), lambda i, j: (0, j)),
                  pl.BlockSpec((tf, d), lambda i, j: (j, 0)),
                  pl.BlockSpec((1, d), lambda i, j: (0, 0))],
        out_specs=pl.BlockSpec((tm, d), lambda i, j: (i, 0)),
        out_shape=jax.ShapeDtypeStruct((t, d), F32),
        scratch_shapes=[pltpu.VMEM((tm, d), BF16), pltpu.VMEM((tm, d), F32)],
        compiler_params=_cparams(("parallel", "arbitrary")),
        name="ffn",
    )(h, gpre, wg, wu, wd, gpost)


def _pack_w_in(w_in):
    d = w_in.shape[0]
    rw = w_in[:, :RW_IN]
    gdn = w_in[:, RW_IN:RW_IN + GDN_IN]
    gates = w_in[:, RW_IN + GDN_IN:]
    z = lambda n: jnp.zeros((d, n), w_in.dtype)
    cols = [gdn[:, :GDN_CONV_CH],
            gates,
            gdn[:, GDN_CONV_CH:GDN_CONV_CH + GDN_V_WIDTH],
            rw, z(RW_BLOCK - RW_IN),
            gdn[:, GDN_CONV_CH + GDN_V_WIDTH:], z(LANES - 4 * GDN_V_HEADS),
            z(P_WIDTH - P_GG - LANES)]
    return _bf(jnp.concatenate(cols, axis=1))


def _block_diag2(a, b):
    z = jnp.zeros_like(a)
    return jnp.concatenate([jnp.concatenate([a, z], axis=1), jnp.concatenate([z, b], axis=1)], axis=0)


def _group_ones(group):
    i = jnp.arange(LANES)
    return (i[:, None] // group == i[None, :] // group).astype(BF16)


def _forward(x, norm_pre_mix, w_in, rw_shift_mu, rw_w0_f, rw_w2_f, rw_w0_b, rw_w2_b, rw_a0_f, rw_a2_f, rw_a0_b,
             rw_a2_b, rw_g2, rw_k_k, rw_k_a, rw_r_k, rw_gn_w, rw_gn_b, gdn_conv_w, gdn_a_log_f, gdn_dt_bias_f,
             gdn_a_log_b, gdn_dt_bias_b, gdn_norm_w, w_branch_rw, w_branch_gdn, w_out, norm_post_mix, norm_pre_ffn,
             w_ffn_gate, w_ffn_up, w_ffn_down, norm_post_ffn, *, tiles):
    batch, seq, d = x.shape
    t = batch * seq
    xf = x.reshape(t, d)
    row = lambda a: a.reshape(1, -1).astype(F32)
    bd64 = _group_ones(RW_HEAD_DIM)
    ones128 = _group_ones(GDN_HEAD_DIM)

    p = _norm_matmul(xf, row(norm_pre_mix), _pack_w_in(w_in), tiles["in_tm"], tiles["in_tn"])

    mu = jnp.pad(row(rw_shift_mu), ((0, 0), (0, RW_BLOCK - RW_IN)))
    w0 = jnp.concatenate([row(rw_w0_f), row(rw_w0_b)], axis=1)
    a0 = jnp.concatenate([row(rw_a0_f), row(rw_a0_b)], axis=1)
    w2 = _bf(_block_diag2(rw_w2_f, rw_w2_b))
    a2 = _bf(_block_diag2(rw_a2_f, rw_a2_b))
    g2 = _bf(jnp.pad(rw_g2, ((0, 2 * LANES - RW_GATE_LORA), (0, 0))))
    v, g, bonus, *scaled = _rw_prep(
        p, seq, tiles["prep_ts"], mu, w0, w2, a0, a2, g2, row(rw_k_k), row(rw_k_a), row(rw_r_k), bd64)
    y_f = _rw_scan(v, *scaled[:5], batch, tiles["rw_tt"], False)
    y_b = _rw_scan(v, *scaled[5:], batch, tiles["rw_tt"], True)
    y_rw = _rw_post(y_f, y_b, bonus, g, row(rw_gn_w), row(rw_gn_b), bd64, tiles["prep_ts"])

    conv_w = jnp.pad(gdn_conv_w.astype(F32), ((0, SUBLANES - GDN_CONV), (0, 0)))
    nv = GDN_V_HEADS
    a_log = jnp.pad(jnp.concatenate([gdn_a_log_f, gdn_a_log_b]).reshape(1, -1), ((0, 0), (0, LANES - 2 * nv)))
    dt_bias = jnp.pad(jnp.concatenate([gdn_dt_bias_f, gdn_dt_bias_b]).reshape(1, -1), ((0, 0), (0, LANES - 2 * nv)))
    q, k, vg, gates = _gdn_prep(p, seq, tiles["prep_ts"], conv_w, a_log.astype(F32), dt_bias.astype(F32), ones128)
    o_f = _gdn_scan(q, k, vg, gates, batch, tiles["gdn_tt"], False)
    o_b = _gdn_scan(q, k, vg, gates, batch, tiles["gdn_tt"], True)
    nw = jnp.tile(row(gdn_norm_w), (1, GDN_V_HEADS))
    y_gdn = _gdn_post(o_f, o_b, p, nw, ones128, tiles["prep_ts"])

    m = _merge(y_rw, y_gdn, p, _bf(w_branch_rw), _bf(w_branch_gdn), tiles["merge_tm"], tiles["merge_tn"])
    h1 = _out_proj(m, xf, _bf(w_out), row(norm_post_mix), tiles["out_tm"])
    h2 = _ffn(h1, row(norm_pre_ffn), _bf(w_ffn_gate), _bf(w_ffn_up), _bf(w_ffn_down), row(norm_post_ffn),
              tiles["ffn_tm"], tiles["ffn_tf"])
    return h2.reshape(batch, seq, d)


def _tiles(seq):
    pick = lambda want: min(want, seq)
    return dict(in_tm=pick(1024), in_tn=1024, prep_ts=pick(256), rw_tt=pick(128), gdn_tt=pick(128),
                merge_tm=pick(512), merge_tn=1024, out_tm=pick(512), ffn_tm=pick(512), ffn_tf=512)


def kernel(x, norm_pre_mix, w_in, rw_shift_mu, rw_w0_f, rw_w2_f, rw_w0_b, rw_w2_b, rw_a0_f, rw_a2_f, rw_a0_b, rw_a2_b, rw_g2, rw_k_k, rw_k_a, rw_r_k, rw_gn_w, rw_gn_b, gdn_conv_w, gdn_a_log_f, gdn_dt_bias_f, gdn_a_log_b, gdn_dt_bias_b, gdn_norm_w, w_branch_rw, w_branch_gdn, w_out, norm_post_mix, norm_pre_ffn, w_ffn_gate, w_ffn_up, w_ffn_down, norm_post_ffn):
    args = [a[0] for a in (norm_pre_mix, w_in, rw_shift_mu, rw_w0_f, rw_w2_f, rw_w0_b, rw_w2_b, rw_a0_f, rw_a2_f,
                           rw_a0_b, rw_a2_b, rw_g2, rw_k_k, rw_k_a, rw_r_k, rw_gn_w, rw_gn_b, gdn_conv_w,
                           gdn_a_log_f, gdn_dt_bias_f, gdn_a_log_b, gdn_dt_bias_b, gdn_norm_w, w_branch_rw,
                           w_branch_gdn, w_out, norm_post_mix, norm_pre_ffn, w_ffn_gate, w_ffn_up, w_ffn_down,
                           norm_post_ffn)]
    assert norm_pre_mix.shape[0] == 1, "one layer"
    return _forward(x, *args, tiles=_tiles(x.shape[1]))
```

```python
import functools

import jax
import jax.numpy as jnp
from jax import lax
from jax.experimental import pallas as pl
from jax.experimental.pallas import tpu as pltpu

F32 = jnp.float32
BF16 = jnp.bfloat16

D_MODEL = 2048
RMS_EPS = 1e-6

RW_HEADS = 16
RW_HEAD_DIM = 64
RW_WIDTH = 1024
RW_LORA = 64
RW_GATE_LORA = 160
RW_GN_EPS = 64e-5
RW_IN = 3488
RW_DECAY_SCALE = 0.6065306597126334

GDN_QK_HEADS = 4
GDN_V_HEADS = 8
GDN_HEAD_DIM = 128
GDN_QK_WIDTH = 512
GDN_V_WIDTH = 1024
GDN_CONV_CH = 2048
GDN_CONV = 5
GDN_NORM_EPS = 1e-6
GDN_IN = 3104
L2_EPS = 1e-6

FFN_HIDDEN = 5632

CHUNK = 64
LANES = 128
SUBLANES = 8

RW_BLOCK = 3584
GDN_BLOCK = 2176
P_MG_RW = 0
P_MG_GDN = 2048
P_GZ = 4096
P_WIDTH = 5120

VMEM_LIMIT = 48 * 1024 * 1024
FRONT_VMEM_LIMIT = 56 * 1024 * 1024


def _cparams(sem):
    return pltpu.CompilerParams(dimension_semantics=sem, vmem_limit_bytes=VMEM_LIMIT)


def _bf(x):
    return x.astype(BF16)


def _mm(a, b):
    return jnp.dot(_bf(a), _bf(b), preferred_element_type=F32)


def _mm_nt(a, b):
    return lax.dot_general(_bf(a), _bf(b), (((1,), (1,)), ((), ())), preferred_element_type=F32)


def _mm_tn(a, b):
    return lax.dot_general(_bf(a), _bf(b), (((0,), (0,)), ((), ())), preferred_element_type=F32)


def _split3(x):
    x1 = _bf(x)
    r1 = x - x1.astype(F32)
    x2 = _bf(r1)
    x3 = _bf(r1 - x2.astype(F32))
    return x1, x2, x3


def _mm01_left(m01, x, terms=3):
    d = functools.partial(jnp.dot, preferred_element_type=F32)
    if terms == 2:
        x1 = _bf(x)
        return d(m01, x1) + d(m01, _bf(x - x1.astype(F32)))
    x1, x2, x3 = _split3(x)
    return d(m01, x1) + d(m01, x2) + d(m01, x3)


def _mm01_right(x, m01):
    return jnp.dot(_bf(x), m01, preferred_element_type=F32)


def _mm01_tn(x, m01):
    x1, x2, x3 = _split3(x)
    d = functools.partial(lax.dot_general, dimension_numbers=(((0,), (0,)), ((), ())), preferred_element_type=F32)
    return d(x1, m01) + d(x2, m01) + d(x3, m01)


def _seg_sum(x, ones_bd):
    blk = ones_bd.shape[0]
    n = x.shape[1] // blk
    parts = [_mm01_right(x[:, j * blk:(j + 1) * blk], ones_bd) for j in range(n)]
    return parts[0] if n == 1 else jnp.concatenate(parts, axis=1)


def _sigmoid(x):
    return 1.0 / (1.0 + jnp.exp(-x))


def _softplus(x):
    return jnp.maximum(x, 0.0) + jnp.log(1.0 + jnp.exp(-jnp.abs(x)))


def _stack_masked(x):
    group = lax.broadcasted_iota(jnp.int32, x.shape, 1) // CHUNK
    return jnp.concatenate([jnp.where(group == h, x, 0.0) for h in range(x.shape[1] // CHUNK)], axis=0)


def _side_masks(reverse, n):
    t = lax.broadcasted_iota(jnp.int32, (CHUNK, n * CHUNK), 0)
    s = lax.broadcasted_iota(jnp.int32, (CHUNK, n * CHUNK), 1) % CHUNK
    if reverse:
        return s > t, s >= t, s == t
    return s < t, s <= t, s == t


def _tri_incl(reverse):
    t = lax.broadcasted_iota(jnp.int32, (CHUNK, CHUNK), 0)
    s = lax.broadcasted_iota(jnp.int32, (CHUNK, CHUNK), 1)
    m = (s >= t) if reverse else (s <= t)
    return jnp.where(m, 1.0, 0.0).astype(BF16)


INVERSE_DOUBLINGS = CHUNK.bit_length() - 3


def _inverse_init(a_side, eye):
    n = -a_side
    return _mm(n, _stack_masked(n)), jnp.where(eye, 1.0, 0.0) + n


def _inverse_double(p, t):
    pt = _mm(jnp.concatenate([p, t], axis=0), _stack_masked(p))
    return pt[:CHUNK], t + pt[CHUNK:]


def _inverse_last(p, t):
    return t + _mm(t, _stack_masked(p))


def _norm_matmul_kernel(x_ref, g_ref, w_ref, o_ref, u_ref):
    @pl.when(pl.program_id(1) == 0)
    def _():
        x = x_ref[...]
        ms = jnp.mean(x * x, axis=-1, keepdims=True)
        u_ref[...] = _bf(x * lax.rsqrt(ms + RMS_EPS) * g_ref[...])

    o_ref[...] = jnp.dot(u_ref[...], w_ref[...], preferred_element_type=F32)


def _norm_matmul(x, gain, w, tm, tn):
    t, d = x.shape
    n = w.shape[1]
    return pl.pallas_call(
        _norm_matmul_kernel,
        grid=(t // tm, n // tn),
        in_specs=[pl.BlockSpec((tm, d), lambda i, j: (i, 0)),
                  pl.BlockSpec((1, d), lambda i, j: (0, 0)),
                  pl.BlockSpec((d, tn), lambda i, j: (0, j))],
        out_specs=[pl.BlockSpec((tm, tn), lambda i, j: (i, j)),
                   pl.BlockSpec((tm, d), lambda i, j: (i, 0))],
        out_shape=[jax.ShapeDtypeStruct((t, n), F32), jax.ShapeDtypeStruct((t, d), BF16)],
        compiler_params=_cparams(("parallel", "arbitrary")),
        name="in_proj",
    )(x, gain, w)


MXU_TILE = 256


HALO = 16


def _projection_pieces(u_ref, uprev_ref, unext_ref, w_ref, p_ref):
    u = jnp.concatenate([uprev_ref[...], u_ref[...], unext_ref[...]], axis=0)
    ncol = w_ref.shape[1]

    def piece(c0):
        c1 = min(c0 + MXU_TILE, ncol)
        p_ref[:, c0:c1] = jnp.dot(u, w_ref[:, c0:c1], preferred_element_type=F32)

    return [functools.partial(piece, c0) for c0 in range(0, ncol, MXU_TILE)]


def _halo_tile(p_ref, pos, tiles_per_seq):
    n = p_ref.shape[0]
    prev = jnp.where(pos == 0, 0.0, p_ref[0:HALO, :])
    nxt = jnp.where(pos == tiles_per_seq - 1, 0.0, p_ref[n - HALO:n, :])
    return jnp.concatenate([prev, p_ref[HALO:n - HALO, :], nxt], axis=0)


def _front_kernel(math, tiles_per_seq, n_params, n_out, *refs):
    u_ref, uprev_ref, unext_ref, w_ref = refs[:4]
    params = refs[4:4 + n_params]
    outs = refs[4 + n_params:4 + n_params + n_out]
    pa_ref, pb_ref = refs[4 + n_params + n_out:]
    i = pl.program_id(0)
    pos = (i + tiles_per_seq - 1) % tiles_per_seq

    @pl.when(i == 0)
    def _():
        pb_ref[...] = jnp.zeros_like(pb_ref)

    def step(p_new, p_old):
        pieces = _projection_pieces(u_ref, uprev_ref, unext_ref, w_ref, p_new)

        per_tick = -(-len(pieces) // (math.ticks + 1))

        def tick():
            for _ in range(min(per_tick, len(pieces))):
                pieces.pop(0)()

        tick()
        math(_halo_tile(p_old, pos, tiles_per_seq), params, outs, tick)
        assert not pieces

    @pl.when(i % 2 == 0)
    def _():
        step(pa_ref, pb_ref)

    @pl.when(i % 2 == 1)
    def _():
        step(pb_ref, pa_ref)


def _front(math, name, u, w, params, param_specs, out_specs, out_shapes, seq, ts):
    t, d = u.shape
    n = t // ts
    nhalo = t // HALO
    rpb = ts // HALO
    cur = lambda i: jnp.minimum(i, n - 1)
    shifted = lambda spec: pl.BlockSpec(spec.block_shape, lambda i, f=spec.index_map: f(jnp.maximum(i - 1, 0)))
    return pl.pallas_call(
        functools.partial(_front_kernel, math, seq // ts, len(params), len(out_specs)),
        grid=(n + 1,),
        in_specs=[pl.BlockSpec((ts, d), lambda i: (cur(i), 0)),
                  pl.BlockSpec((HALO, d), lambda i: (jnp.maximum(cur(i) * rpb - 1, 0), 0)),
                  pl.BlockSpec((HALO, d), lambda i: (jnp.minimum((cur(i) + 1) * rpb, nhalo - 1), 0)),
                  pl.BlockSpec(w.shape, lambda i: (0, 0), pipeline_mode=pl.Buffered(1))] + param_specs,
        out_specs=[shifted(s) for s in out_specs],
        out_shape=out_shapes,
        scratch_shapes=[pltpu.VMEM((ts + 2 * HALO, w.shape[1]), F32)] * 2,
        compiler_params=pltpu.CompilerParams(dimension_semantics=("arbitrary",), vmem_limit_bytes=FRONT_VMEM_LIMIT),
        name=name,
    )(u, u, u, w, *params)


def _rw_prep_math(ext, params, outs, tick):
    mu_ref, w0_ref, w2_ref, a0_ref, a2_ref, g2_ref, kk_ref, ka_ref, rk_ref, bd_ref = params
    v_o, g_o, bonus_o, rf_o, kf_o, bf_o, kkf_o, gcf_o, rb_o, kb_o, bb_o, kkb_o, gcb_o = outs
    n = ext.shape[0]
    ts = n - 2 * HALO
    w = RW_WIDTH

    def mixed(c0, c1):
        e = ext[:, c0:c1]
        body = slice(HALO, HALO + ts)
        mu = mu_ref[:, c0:c1]
        out = e[body] * (1.0 - mu) + (pltpu.roll(e, 1, axis=0)[body] + pltpu.roll(e, n - 1, axis=0)[body]) * (0.5 * mu)
        tick()
        return out

    lora = mixed(3 * w, RW_BLOCK)
    wlin = _mm(jnp.tanh(lora[:, 0:LANES]), w2_ref[...]) + w0_ref[...]
    alin = _mm(lora[:, LANES:2 * LANES], a2_ref[...]) + a0_ref[...]
    g = _mm(_sigmoid(lora[:, 2 * LANES:4 * LANES]), g2_ref[...])
    r = mixed(0, w)
    k = mixed(w, 2 * w)
    v = mixed(2 * w, 3 * w)
    lw = -RW_DECAY_SCALE * _sigmoid(wlin)
    tick()
    a = _sigmoid(alin)
    tick()

    bd = bd_ref[...]
    kscaled = k * kk_ref[...]
    kkn = kscaled * lax.rsqrt(_seg_sum(kscaled * kscaled, bd) + L2_EPS)
    tick()
    ka = ka_ref[...]
    a_f = a[:, 0:w]
    a_b = a[:, w:2 * w]
    k_f = k * (1.0 + (a_f - 1.0) * ka)
    k_b = k * (1.0 + (a_b - 1.0) * ka)
    tick()
    bonus = _seg_sum(r * (0.5 * (k_f + k_b)) * rk_ref[...], bd) * v

    v_o[...] = _bf(v)
    g_o[...] = g
    bonus_o[...] = bonus
    tick()

    rt = lax.broadcasted_iota(jnp.int32, (ts, ts), 0)
    rs = lax.broadcasted_iota(jnp.int32, (ts, ts), 1)
    same_chunk = (rt // CHUNK) == (rs // CHUNK)
    for d, (k_d, a_d, r_o, k_o, b_o, kk_o, gc_o) in enumerate(((k_f, a_f, rf_o, kf_o, bf_o, kkf_o, gcf_o),
                                                                (k_b, a_b, rb_o, kb_o, bb_o, kkb_o, gcb_o))):
        lw_d = lw[:, d * w:(d + 1) * w]
        before = (rs <= rt) if d == 0 else (rs >= rt)
        cum = _mm01_left(jnp.where(same_chunk & before, 1.0, 0.0).astype(BF16), lw_d, terms=2)
        tick()
        g_inv = jnp.exp(-cum)
        r_o[...] = _bf(r * jnp.exp(cum))
        tick()
        k_o[...] = _bf(k_d * g_inv)
        b_o[...] = _bf(kkn * a_d * g_inv)
        tick()
        kk_o[...] = _bf(kkn * jnp.exp(cum - lw_d))
        for c in range(ts // CHUNK):
            last = c * CHUNK + (CHUNK - 1 if d == 0 else 0)
            gc_o[c] = jnp.exp(cum[last:last + 1, :])
        tick()


_rw_prep_math.ticks = 17


def _rw_front(u, w_rw, seq, ts, mu, w0, w2, a0, a2, g2, k_k, k_a, r_k, bd64):
    t = u.shape[0]
    params = [mu, w0, w2, a0, a2, g2, k_k, k_a, r_k, bd64]
    cpt = ts // CHUNK
    big = pl.BlockSpec((ts, RW_WIDTH), lambda i: (i, 0))
    tot = pl.BlockSpec((cpt, 1, RW_WIDTH), lambda i: (i, 0, 0))
    wide = lambda dt: jax.ShapeDtypeStruct((t, RW_WIDTH), dt)
    tots = jax.ShapeDtypeStruct((t // CHUNK, 1, RW_WIDTH), F32)
    direction = [wide(BF16)] * 4 + [tots]
    return _front(_rw_prep_math, "rw_front", u, w_rw, params,
                  [pl.BlockSpec(a.shape, lambda i: (0, 0)) for a in params],
                  [big] * 3 + ([big] * 4 + [tot]) * 2,
                  [wide(BF16), wide(F32), wide(F32)] + direction * 2, seq, ts)


RW_GROUP = 4
RW_GROUP_W = RW_GROUP * RW_HEAD_DIM


def _rw_scan_kernel(reverse, v_ref, r_ref, k_ref, b_ref, kk_ref, gc_ref, y_ref, s_ref):
    @pl.when(pl.program_id(0) == 0)
    def _():
        s_ref[...] = jnp.zeros_like(s_ref)

    gw = RW_GROUP_W
    nb, rows, width = r_ref.shape
    nblk = width // gw
    nch = rows // CHUNK
    strict, incl, eye = _side_masks(reverse, RW_GROUP)
    ri = lax.broadcasted_iota(jnp.int32, (gw, gw), 0) // CHUNK
    ci = lax.broadcasted_iota(jnp.int32, (gw, gw), 1) // CHUNK
    bd_mask = ri == ci
    order = list(range(nch - 1, -1, -1) if reverse else range(nch))
    lanes = [(bi, blk) for bi in range(nb) for blk in range(nblk)]
    chains = [(bi, blk, c) for c in order for (bi, blk) in lanes]

    def tile(ref, bi, blk, c):
        return ref[bi, c * CHUNK:(c + 1) * CHUNK, blk * gw:(blk + 1) * gw]

    r_t = {ch: tile(r_ref, *ch) for ch in chains}
    k_t = {ch: tile(k_ref, *ch) for ch in chains}
    b_t = {ch: tile(b_ref, *ch) for ch in chains}
    kk_t = {ch: tile(kk_ref, *ch) for ch in chains}
    vv = {ch: tile(v_ref, *ch) for ch in chains}

    a_v, a_kb, a_rb, pp, tt = {}, {}, {}, {}, {}
    for ch in chains:
        lhs = jnp.concatenate([kk_t[ch], r_t[ch]], axis=0)
        a_k = _mm_nt(lhs, _stack_masked(k_t[ch]))
        a_b = _mm_nt(lhs, _stack_masked(b_t[ch]))
        a_v[ch] = jnp.concatenate([jnp.where(strict, a_k[:CHUNK], 0.0), jnp.where(incl, a_k[CHUNK:], 0.0)], axis=0)
        a_rb[ch] = jnp.where(incl, a_b[CHUNK:], 0.0)
        a_kb[ch] = jnp.where(strict, a_b[:CHUNK], 0.0)
    for ch in chains:
        pp[ch], tt[ch] = _inverse_init(a_kb[ch], eye)
    for _ in range(INVERSE_DOUBLINGS):
        for ch in chains:
            pp[ch], tt[ch] = _inverse_double(pp[ch], tt[ch])
    for ch in chains:
        tt[ch] = _inverse_last(pp[ch], tt[ch])

    w_mat, u0, y0 = {}, {}, {}
    for ch in chains:
        av = _mm(a_v[ch], _stack_masked(vv[ch]))
        y0[ch] = av[CHUNK:]
        u0[ch] = av[:CHUNK]
        w_mat[ch] = _mm(tt[ch], _stack_masked(kk_t[ch]))
    for ch in chains:
        u0[ch] = _mm(tt[ch], _stack_masked(u0[ch]))

    state = {(bi, blk): s_ref[bi * nblk + blk] for (bi, blk) in lanes}
    for c in order:
        ws = {ln: _mm_nt(jnp.concatenate([_bf(w_mat[(*ln, c)]), r_t[(*ln, c)]], axis=0), state[ln]) for ln in lanes}
        u = {ln: ws[ln][:CHUNK] + u0[(*ln, c)] for ln in lanes}
        ds = {ln: _mm_tn(jnp.concatenate([vv[(*ln, c)], _bf(-u[ln])], axis=0),
                         jnp.concatenate([k_t[(*ln, c)], b_t[(*ln, c)]], axis=0)) for ln in lanes}
        for (bi, blk) in lanes:
            ln = (bi, blk)
            state[ln] = (state[ln] + jnp.where(bd_mask, ds[ln], 0.0)) * gc_ref[bi, c, :, blk * gw:(blk + 1) * gw]
            y = ws[ln][CHUNK:] + y0[(bi, blk, c)] - _mm(a_rb[(bi, blk, c)], _stack_masked(u[ln]))
            y_ref[bi, c * CHUNK:(c + 1) * CHUNK, blk * gw:(blk + 1) * gw] = y
    for (bi, blk) in lanes:
        s_ref[bi * nblk + blk] = state[(bi, blk)]


def _rw_scan(v, r_t, k_t, b_t, kk_t, gc, batch, tt, reverse):
    t = v.shape[0]
    seq = t // batch
    nt = seq // tt
    cpt = tt // CHUNK
    rowi = (lambda c: nt - 1 - c) if reverse else (lambda c: c)
    spec = pl.BlockSpec((batch, tt, RW_WIDTH), lambda c: (0, rowi(c), 0))
    seqs = lambda a: a.reshape(batch, seq, RW_WIDTH)
    y = pl.pallas_call(
        functools.partial(_rw_scan_kernel, reverse),
        grid=(nt,),
        in_specs=[spec] * 5 + [pl.BlockSpec((batch, cpt, 1, RW_WIDTH), lambda c: (0, rowi(c), 0, 0))],
        out_specs=spec,
        out_shape=jax.ShapeDtypeStruct((batch, seq, RW_WIDTH), F32),
        scratch_shapes=[pltpu.VMEM((batch * RW_WIDTH // RW_GROUP_W, RW_GROUP_W, RW_GROUP_W), F32)],
        compiler_params=_cparams(("arbitrary",)),
        name="rw_scan_bwd" if reverse else "rw_scan_fwd",
    )(seqs(v), seqs(r_t), seqs(k_t), seqs(b_t), seqs(kk_t), gc.reshape(batch, seq // CHUNK, 1, RW_WIDTH))
    return y.reshape(t, RW_WIDTH)


def _rw_post_math(y_f, y_b, bonus, g, gn_w, gn_b, bd):
    y = y_f + y_b
    inv_n = 1.0 / RW_HEAD_DIM
    mean = _seg_sum(y, bd) * inv_n
    yc = y - mean
    var = _seg_sum(yc * yc, bd) * inv_n
    yn = yc * lax.rsqrt(var + RW_GN_EPS) * gn_w + gn_b
    return _bf((yn + bonus) * g)


def _gdn_prep_math(ext_all, params, outs, tick):
    cw_ref, alog_ref, dtb_ref, ones_ref = params
    q_o, k_o, v_o, gates_o = outs
    n = ext_all.shape[0]
    ts = n - 2 * HALO
    half = GDN_CONV // 2

    def conv_silu(c0, c1):
        e = ext_all[:, c0:c1]
        acc = None
        for j in range(GDN_CONV):
            shift = (half - j) % n
            xs = e if shift == 0 else pltpu.roll(e, shift, axis=0)
            term = cw_ref[j:j + 1, c0:c1] * xs[HALO:HALO + ts, :]
            acc = term if acc is None else acc + term
        tick()
        out = acc * _sigmoid(acc)
        tick()
        return out

    ones = ones_ref[...]
    qw = GDN_QK_WIDTH
    q = conv_silu(0, qw)
    q_o[...] = q * lax.rsqrt(_seg_sum(q * q, ones) + L2_EPS) * (GDN_HEAD_DIM ** -0.5)
    k = conv_silu(qw, 2 * qw)
    k_o[...] = k * lax.rsqrt(_seg_sum(k * k, ones) + L2_EPS)
    v_o[:, 0:qw] = conv_silu(2 * qw, 3 * qw)
    v_o[:, qw:2 * qw] = conv_silu(3 * qw, 4 * qw)

    gx = ext_all[HALO:HALO + ts, GDN_CONV_CH:]
    lane = lax.broadcasted_iota(jnp.int32, gx.shape, 1)
    log_decay = -jnp.exp(alog_ref[...]) * _softplus(gx + dtb_ref[...])
    gates_o[...] = jnp.where(lane < 2 * GDN_V_HEADS, log_decay, _sigmoid(gx))


_gdn_prep_math.ticks = 8


def _gdn_front(u, w_gdn, seq, ts, conv_w, a_log, dt_bias, ones128):
    t = u.shape[0]
    params = [conv_w, a_log, dt_bias, ones128]
    widths = (GDN_QK_WIDTH, GDN_QK_WIDTH, GDN_V_WIDTH, LANES)
    return _front(_gdn_prep_math, "gdn_front", u, w_gdn, params,
                  [pl.BlockSpec(a.shape, lambda i: (0, 0)) for a in params],
                  [pl.BlockSpec((ts, wd), lambda i: (i, 0)) for wd in widths],
                  [jax.ShapeDtypeStruct((t, wd), F32) for wd in widths], seq, ts)


def _gdn_scan_kernel(reverse, g_off, b_off, q_ref, k_ref, v_ref, gates_ref, o_ref, s_ref):
    @pl.when(pl.program_id(0) == 0)
    def _():
        s_ref[...] = jnp.zeros_like(s_ref)

    nb = q_ref.shape[0]
    nch = q_ref.shape[1] // CHUNK
    strict, incl, _ = _side_masks(reverse, 2)
    eye4 = _side_masks(reverse, 4)[2]
    tri = _tri_incl(reverse)
    s_i = lax.broadcasted_iota(jnp.int32, (CHUNK, 2 * CHUNK), 0)
    t_i = lax.broadcasted_iota(jnp.int32, (CHUNK, 2 * CHUNK), 1) % CHUNK
    tri_t2 = jnp.where((s_i >= t_i) if reverse else (s_i <= t_i), 1.0, 0.0).astype(BF16)
    first = lax.broadcasted_iota(jnp.int32, (CHUNK, 2 * CHUNK), 1) < CHUNK
    zero = jnp.zeros((CHUNK, LANES), F32)
    zero_s = jnp.zeros((LANES, LANES), F32)
    last = 0 if reverse else CHUNK - 1
    order = list(range(nch - 1, -1, -1) if reverse else range(nch))
    slots = [(bi, c) for c in order for bi in range(nb)]
    chains = [(bi, j, c) for (bi, c) in slots for j in range(GDN_QK_HEADS)]

    def rows(c):
        return slice(c * CHUNK, (c + 1) * CHUNK)

    gates, ccol, crow = {}, {}, {}
    for sl in slots:
        gates[sl] = gates_ref[sl[0], rows(sl[1]), :]
        ccol[sl] = _mm01_left(tri, gates[sl])
        crow[sl] = _mm01_tn(gates[sl], tri_t2)

    q, k, qk_pair, a_pair = {}, {}, {}, {}
    for ch in chains:
        bi, j, c = ch
        sl = (bi, c)
        i0, i1 = g_off + 2 * j, g_off + 2 * j + 1
        q[ch] = q_ref[bi, rows(c), j * LANES:(j + 1) * LANES]
        k[ch] = k_ref[bi, rows(c), j * LANES:(j + 1) * LANES]
        diff = (jnp.where(first, ccol[sl][:, i0:i0 + 1], ccol[sl][:, i1:i1 + 1])
                - jnp.where(first, crow[sl][i0:i0 + 1, :], crow[sl][i1:i1 + 1, :]))
        decay = jnp.where(incl, jnp.exp(jnp.minimum(diff, 0.0)), 0.0)
        beta_pair = jnp.where(first, gates[sl][:, b_off + 2 * j:b_off + 2 * j + 1],
                              gates[sl][:, b_off + 2 * j + 1:b_off + 2 * j + 2])
        kdup = jnp.concatenate([k[ch], k[ch]], axis=0)
        kq = _mm_nt(jnp.concatenate([k[ch], q[ch]], axis=0), kdup)
        qk_pair[ch] = kq[CHUNK:] * decay
        a_pair[ch] = jnp.where(strict, beta_pair * kq[:CHUNK] * decay, 0.0)

    quads = [(bi, jj, c) for (bi, c) in slots for jj in range(GDN_QK_HEADS // 2)]
    pp, tq = {}, {}
    for (bi, jj, c) in quads:
        pp[(bi, jj, c)], tq[(bi, jj, c)] = _inverse_init(
            jnp.concatenate([a_pair[(bi, 2 * jj, c)], a_pair[(bi, 2 * jj + 1, c)]], axis=1), eye4)
    for _ in range(INVERSE_DOUBLINGS):
        for qd in quads:
            pp[qd], tq[qd] = _inverse_double(pp[qd], tq[qd])
    tt = {}
    for (bi, jj, c) in quads:
        t4 = _inverse_last(pp[(bi, jj, c)], tq[(bi, jj, c)])
        tt[(bi, 2 * jj, c)] = t4[:, :LANES]
        tt[(bi, 2 * jj + 1, c)] = t4[:, LANES:]

    sol, wq = {}, {}
    for ch in chains:
        bi, j, c = ch
        parts = []
        for e in range(2):
            h = 2 * j + e
            beta = gates[(bi, c)][:, b_off + h:b_off + h + 1]
            eg = jnp.exp(ccol[(bi, c)][:, g_off + h:g_off + h + 1])
            v_e = v_ref[bi, rows(c), h * LANES:(h + 1) * LANES]
            blocks = [zero] * 4
            blocks[2 * e] = v_e * beta
            blocks[2 * e + 1] = k[ch] * (beta * eg)
            parts.append(jnp.concatenate(blocks, axis=1))
            wq[(bi, j, c, e)] = q[ch] * eg
        sol[ch] = _mm(tt[ch], jnp.concatenate(parts, axis=0))

    pairs = [(bi, j) for bi in range(nb) for j in range(GDN_QK_HEADS)]
    heads = [(bi, j, e) for (bi, j) in pairs for e in range(2)]
    state = {(bi, j, e): s_ref[(bi * GDN_QK_HEADS + j) * 2 + e] for (bi, j, e) in heads}
    for c in order:
        ws = {}
        for (bi, j) in pairs:
            so = sol[(bi, j, c)]
            lhs = jnp.concatenate([jnp.concatenate([so[:, LANES:2 * LANES], so[:, 3 * LANES:]], axis=1),
                                   jnp.concatenate([wq[(bi, j, c, 0)], wq[(bi, j, c, 1)]], axis=1)], axis=0)
            s_bd = jnp.concatenate([jnp.concatenate([state[(bi, j, 0)], zero_s], axis=1),
                                    jnp.concatenate([zero_s, state[(bi, j, 1)]], axis=1)], axis=0)
            ws[(bi, j)] = _mm(lhs, s_bd)
        vn = {(bi, j, e): (sol[(bi, j, c)][:, 2 * e * LANES:(2 * e + 1) * LANES]
                           - ws[(bi, j)][:CHUNK, e * LANES:(e + 1) * LANES]) for (bi, j, e) in heads}
        for (bi, j, e) in heads:
            cc = ccol[(bi, c)][:, g_off + 2 * j + e:g_off + 2 * j + e + 1]
            tot = cc[last:last + 1, :]
            kg = k[(bi, j, c)] * jnp.exp(tot - cc)
            state[(bi, j, e)] = state[(bi, j, e)] * jnp.exp(tot) + _mm_tn(kg, vn[(bi, j, e)])
        for (bi, j) in pairs:
            vn_bd = jnp.concatenate([jnp.concatenate([vn[(bi, j, 0)], zero], axis=1),
                                     jnp.concatenate([zero, vn[(bi, j, 1)]], axis=1)], axis=0)
            o_intra = _mm(qk_pair[(bi, j, c)], vn_bd)
            o_ref[bi, rows(c), 2 * j * LANES:(2 * j + 2) * LANES] = ws[(bi, j)][CHUNK:] + o_intra
    for (bi, j, e) in heads:
        s_ref[(bi * GDN_QK_HEADS + j) * 2 + e] = state[(bi, j, e)]


def _gdn_scan(q, k, v, gates, batch, tt, reverse):
    t = q.shape[0]
    seq = t // batch
    nt = seq // tt
    rowi = (lambda c: nt - 1 - c) if reverse else (lambda c: c)
    g_off = GDN_V_HEADS if reverse else 0
    b_off = 2 * GDN_V_HEADS + g_off
    spec = lambda width: pl.BlockSpec((batch, tt, width), lambda c: (0, rowi(c), 0))
    seqs = lambda a: a.reshape(batch, seq, a.shape[-1])
    o = pl.pallas_call(
        functools.partial(_gdn_scan_kernel, reverse, g_off, b_off),
        grid=(nt,),
        in_specs=[spec(GDN_QK_WIDTH), spec(GDN_QK_WIDTH), spec(GDN_V_WIDTH), spec(LANES)],
        out_specs=spec(GDN_V_WIDTH),
        out_shape=jax.ShapeDtypeStruct((batch, seq, GDN_V_WIDTH), F32),
        scratch_shapes=[pltpu.VMEM((batch * GDN_V_HEADS, LANES, LANES), F32)],
        compiler_params=_cparams(("arbitrary",)),
        name="gdn_scan_bwd" if reverse else "gdn_scan_fwd",
    )(seqs(q), seqs(k), seqs(v), seqs(gates))
    return o.reshape(t, GDN_V_WIDTH)


def _gdn_post_math(o_f, o_b, z, norm_w, ones):
    o = o_f + o_b
    ms = _seg_sum(o * o, ones) * (1.0 / GDN_HEAD_DIM)
    return _bf(o * lax.rsqrt(ms + GDN_NORM_EPS) * norm_w * (z * _sigmoid(z)))


MERGE_SUB = 128


def _merge_kernel(yf_ref, yb_ref, bonus_ref, g_ref, gnw_ref, gnb_ref, bd_ref,
                  of_ref, ob_ref, z_ref, nw_ref, ones_ref, gr_ref, gg_ref, pa_ref, pb_ref, o_ref):
    for s in range(yf_ref.shape[0] // MERGE_SUB):
        rows = slice(s * MERGE_SUB, (s + 1) * MERGE_SUB)
        y_rw = _rw_post_math(yf_ref[rows, :], yb_ref[rows, :], bonus_ref[rows, :], g_ref[rows, :],
                             gnw_ref[...], gnb_ref[...], bd_ref[...])
        y_gdn = _gdn_post_math(of_ref[rows, :], ob_ref[rows, :], z_ref[rows, :], nw_ref[...], ones_ref[...])
        a = jnp.dot(y_rw, pa_ref[...], preferred_element_type=F32)
        b = jnp.dot(y_gdn, pb_ref[...], preferred_element_type=F32)
        o_ref[rows, :] = _bf(_sigmoid(gr_ref[rows, :]) * a + _sigmoid(gg_ref[rows, :]) * b)


def _merge(y_f, y_b, bonus, g, gn_w, gn_b, bd64, o_f, o_b, p, norm_w, ones128, w_a, w_b, tm):
    t = y_f.shape[0]
    wide = pl.BlockSpec((tm, RW_WIDTH), lambda i: (i, 0))
    row = pl.BlockSpec((1, RW_WIDTH), lambda i: (0, 0))
    const = lambda a: pl.BlockSpec(a.shape, lambda i: (0, 0))
    pcol = lambda width, off: pl.BlockSpec((tm, width), lambda i: (i, off // width))
    return pl.pallas_call(
        _merge_kernel,
        grid=(t // tm,),
        in_specs=[wide, wide, wide, wide, row, row, const(bd64),
                  wide, wide, pcol(GDN_V_WIDTH, P_GZ), row, const(ones128),
                  pcol(D_MODEL, P_MG_RW), pcol(D_MODEL, P_MG_GDN), const(w_a), const(w_b)],
        out_specs=pl.BlockSpec((tm, D_MODEL), lambda i: (i, 0)),
        out_shape=jax.ShapeDtypeStruct((t, D_MODEL), BF16),
        compiler_params=_cparams(("parallel",)),
        name="merge",
    )(y_f, y_b, bonus, g, gn_w, gn_b, bd64, o_f, o_b, p, norm_w, ones128, p, p, w_a, w_b)


def _out_proj_kernel(m_ref, x_ref, w_ref, g_ref, gnext_ref, h_ref, u_ref):
    y = jnp.dot(m_ref[...], w_ref[...], preferred_element_type=F32)
    ms = jnp.mean(y * y, axis=-1, keepdims=True)
    h = x_ref[...] + y * lax.rsqrt(ms + RMS_EPS) * g_ref[...]
    h_ref[...] = h
    ms2 = jnp.mean(h * h, axis=-1, keepdims=True)
    u_ref[...] = _bf(h * lax.rsqrt(ms2 + RMS_EPS) * gnext_ref[...])


def _out_proj(m, x, w, gain, gain_next, tm):
    t, d = x.shape
    tile = pl.BlockSpec((tm, d), lambda i: (i, 0))
    row = pl.BlockSpec((1, d), lambda i: (0, 0))
    return pl.pallas_call(
        _out_proj_kernel,
        grid=(t // tm,),
        in_specs=[tile, tile, pl.BlockSpec((d, d), lambda i: (0, 0)), row, row],
        out_specs=[tile, tile],
        out_shape=[jax.ShapeDtypeStruct((t, d), F32), jax.ShapeDtypeStruct((t, d), BF16)],
        compiler_params=_cparams(("parallel",)),
        name="out_proj",
    )(m, x, w, gain, gain_next)


def _ffn_kernel(h_ref, u_ref, wg_ref, wu_ref, wd_ref, gpost_ref, o_ref):
    j = pl.program_id(1)

    @pl.when(j == 0)
    def _():
        o_ref[...] = jnp.zeros_like(o_ref)

    u = u_ref[...]
    gate = jnp.dot(u, wg_ref[...], preferred_element_type=F32)
    up = jnp.dot(u, wu_ref[...], preferred_element_type=F32)
    f = _bf(gate * _sigmoid(gate) * up)
    o_ref[...] += jnp.dot(f, wd_ref[...], preferred_element_type=F32)

    @pl.when(j == pl.num_programs(1) - 1)
    def _():
        y = o_ref[...]
        ms = jnp.mean(y * y, axis=-1, keepdims=True)
        o_ref[...] = h_ref[...] + y * lax.rsqrt(ms + RMS_EPS) * gpost_ref[...]


def _ffn(h, u, wg, wu, wd, gpost, tm, tf):
    t, d = h.shape
    f = wg.shape[1]
    return pl.pallas_call(
        _ffn_kernel,
        grid=(t // tm, f // tf),
        in_specs=[pl.BlockSpec((tm, d), lambda i, j: (i, 0)),
                  pl.BlockSpec((tm, d), lambda i, j: (i, 0)),
                  pl.BlockSpec((d, tf), lambda i, j: (0, j)),
                  pl.BlockSpec((d, tf), lambda i, j: (0, j)),
                  pl.BlockSpec((tf, d), lambda i, j: (j, 0)),
                  pl.BlockSpec((1, d), lambda i, j: (0, 0))],
        out_specs=pl.BlockSpec((tm, d), lambda i, j: (i, 0)),
        out_shape=jax.ShapeDtypeStruct((t, d), F32),
        compiler_params=_cparams(("parallel", "arbitrary")),
        name="ffn",
    )(h, u, wg, wu, wd, gpost)


def _pack_w_in(w_in):
    d = w_in.shape[0]
    rw = w_in[:, :RW_IN]
    gdn = w_in[:, RW_IN:RW_IN + GDN_IN]
    gates = w_in[:, RW_IN + GDN_IN:]
    z = lambda n: jnp.zeros((d, n), w_in.dtype)
    cat = lambda cols: _bf(jnp.concatenate(cols, axis=1))
    w_rw = cat([rw, z(RW_BLOCK - RW_IN)])
    w_gdn = cat([gdn[:, :GDN_CONV_CH], gdn[:, GDN_CONV_CH + GDN_V_WIDTH:], z(LANES - 4 * GDN_V_HEADS)])
    w_rest = cat([gates, gdn[:, GDN_CONV_CH:GDN_CONV_CH + GDN_V_WIDTH]])
    return w_rw, w_gdn, w_rest


def _block_diag2(a, b):
    z = jnp.zeros_like(a)
    return jnp.concatenate([jnp.concatenate([a, z], axis=1), jnp.concatenate([z, b], axis=1)], axis=0)


def _group_ones(group):
    i = jnp.arange(MXU_TILE)
    return (i[:, None] // group == i[None, :] // group).astype(BF16)


def _forward(x, norm_pre_mix, w_in, rw_shift_mu, rw_w0_f, rw_w2_f, rw_w0_b, rw_w2_b, rw_a0_f, rw_a2_f, rw_a0_b,
             rw_a2_b, rw_g2, rw_k_k, rw_k_a, rw_r_k, rw_gn_w, rw_gn_b, gdn_conv_w, gdn_a_log_f, gdn_dt_bias_f,
             gdn_a_log_b, gdn_dt_bias_b, gdn_norm_w, w_branch_rw, w_branch_gdn, w_out, norm_post_mix, norm_pre_ffn,
             w_ffn_gate, w_ffn_up, w_ffn_down, norm_post_ffn, *, tiles):
    batch, seq, d = x.shape
    t = batch * seq
    xf = x.reshape(t, d)
    row = lambda a: a.reshape(1, -1).astype(F32)
    bd64 = _group_ones(RW_HEAD_DIM)
    ones128 = _group_ones(GDN_HEAD_DIM)

    w_rw, w_gdn, w_rest = _pack_w_in(w_in)
    p, u_in = _norm_matmul(xf, row(norm_pre_mix), w_rest, tiles["in_tm"], tiles["in_tn"])

    mu = jnp.pad(row(rw_shift_mu), ((0, 0), (0, RW_BLOCK - RW_IN)))
    w0 = jnp.concatenate([row(rw_w0_f), row(rw_w0_b)], axis=1)
    a0 = jnp.concatenate([row(rw_a0_f), row(rw_a0_b)], axis=1)
    w2 = _bf(_block_diag2(rw_w2_f, rw_w2_b))
    a2 = _bf(_block_diag2(rw_a2_f, rw_a2_b))
    g2 = _bf(jnp.pad(rw_g2, ((0, 2 * LANES - RW_GATE_LORA), (0, 0))))
    v, g, bonus, *scaled = _rw_front(
        u_in, w_rw, seq, tiles["prep_ts"], mu, w0, w2, a0, a2, g2, row(rw_k_k), row(rw_k_a), row(rw_r_k), bd64)
    y_f = _rw_scan(v, *scaled[:5], batch, tiles["rw_tt"], False)
    y_b = _rw_scan(v, *scaled[5:], batch, tiles["rw_tt"], True)

    conv_w = jnp.pad(gdn_conv_w.astype(F32), ((0, SUBLANES - GDN_CONV), (0, 0)))
    nv = GDN_V_HEADS
    a_log = jnp.pad(jnp.concatenate([gdn_a_log_f, gdn_a_log_b]).reshape(1, -1), ((0, 0), (0, LANES - 2 * nv)))
    dt_bias = jnp.pad(jnp.concatenate([gdn_dt_bias_f, gdn_dt_bias_b]).reshape(1, -1), ((0, 0), (0, LANES - 2 * nv)))
    q, k, vg, gates = _gdn_front(u_in, w_gdn, seq, tiles["prep_ts"], conv_w, a_log.astype(F32),
                                 dt_bias.astype(F32), ones128)
    o_f = _gdn_scan(q, k, vg, gates, batch, tiles["gdn_tt"], False)
    o_b = _gdn_scan(q, k, vg, gates, batch, tiles["gdn_tt"], True)
    nw = jnp.tile(row(gdn_norm_w), (1, GDN_V_HEADS))

    m = _merge(y_f, y_b, bonus, g, row(rw_gn_w), row(rw_gn_b), bd64, o_f, o_b, p, nw, ones128,
               _bf(w_branch_rw), _bf(w_branch_gdn), tiles["merge_tm"])
    h1, u_ffn = _out_proj(m, xf, _bf(w_out), row(norm_post_mix), row(norm_pre_ffn), tiles["out_tm"])
    h2 = _ffn(h1, u_ffn, _bf(w_ffn_gate), _bf(w_ffn_up), _bf(w_ffn_down), row(norm_post_ffn),
              tiles["ffn_tm"], tiles["ffn_tf"])
    return h2.reshape(batch, seq, d)


def _tiles(seq):
    pick = lambda want: min(want, seq)
    return dict(in_tm=pick(1024), in_tn=1024, prep_ts=pick(256), rw_tt=pick(128), gdn_tt=pick(128),
                merge_tm=pick(256), out_tm=pick(512), ffn_tm=pick(512), ffn_tf=512)


def kernel(x, norm_pre_mix, w_in, rw_shift_mu, rw_w0_f, rw_w2_f, rw_w0_b, rw_w2_b, rw_a0_f, rw_a2_f, rw_a0_b, rw_a2_b, rw_g2, rw_k_k, rw_k_a, rw_r_k, rw_gn_w, rw_gn_b, gdn_conv_w, gdn_a_log_f, gdn_dt_bias_f, gdn_a_log_b, gdn_dt_bias_b, gdn_norm_w, w_branch_rw, w_branch_gdn, w_out, norm_post_mix, norm_pre_ffn, w_ffn_gate, w_ffn_up, w_ffn_down, norm_post_ffn):
    args = [a[0] for a in (norm_pre_mix, w_in, rw_shift_mu, rw_w0_f, rw_w2_f, rw_w0_b, rw_w2_b, rw_a0_f, rw_a2_f,
                           rw_a0_b, rw_a2_b, rw_g2, rw_k_k, rw_k_a, rw_r_k, rw_gn_w, rw_gn_b, gdn_conv_w,
                           gdn_a_log_f, gdn_dt_bias_f, gdn_a_log_b, gdn_dt_bias_b, gdn_norm_w, w_branch_rw,
                           w_branch_gdn, w_out, norm_post_mix, norm_pre_ffn, w_ffn_gate, w_ffn_up, w_ffn_down,
                           norm_post_ffn)]
    assert norm_pre_mix.shape[0] == 1, "one layer"
    return _forward(x, *args, tiles=_tiles(x.shape[1]))
```

```python
import functools

import jax
import jax.numpy as jnp
from jax import lax
from jax.experimental import pallas as pl
from jax.experimental.pallas import tpu as pltpu

F32 = jnp.float32
BF16 = jnp.bfloat16

D_MODEL = 2048
RMS_EPS = 1e-6

RW_HEADS = 16
RW_HEAD_DIM = 64
RW_WIDTH = 1024
RW_LORA = 64
RW_GATE_LORA = 160
RW_GN_EPS = 64e-5
RW_IN = 3488
RW_DECAY_SCALE = 0.6065306597126334

GDN_QK_HEADS = 4
GDN_V_HEADS = 8
GDN_HEAD_DIM = 128
GDN_QK_WIDTH = 512
GDN_V_WIDTH = 1024
GDN_CONV_CH = 2048
GDN_CONV = 5
GDN_NORM_EPS = 1e-6
GDN_IN = 3104
L2_EPS = 1e-6

FFN_HIDDEN = 5632

CHUNK = 64
LANES = 128
SUBLANES = 8

RW_BLOCK = 3584
GDN_BLOCK = 2176
P_MG_RW = 0
P_MG_GDN = 2048
P_GZ = 4096
P_WIDTH = 5120

VMEM_LIMIT = 48 * 1024 * 1024
FRONT_VMEM_LIMIT = 56 * 1024 * 1024


def _cparams(sem):
    return pltpu.CompilerParams(dimension_semantics=sem, vmem_limit_bytes=VMEM_LIMIT)


def _bf(x):
    return x.astype(BF16)


def _mm(a, b):
    return jnp.dot(_bf(a), _bf(b), preferred_element_type=F32)


def _mm_nt(a, b):
    return lax.dot_general(_bf(a), _bf(b), (((1,), (1,)), ((), ())), preferred_element_type=F32)


def _mm_tn(a, b):
    return lax.dot_general(_bf(a), _bf(b), (((0,), (0,)), ((), ())), preferred_element_type=F32)


def _split3(x):
    x1 = _bf(x)
    r1 = x - x1.astype(F32)
    x2 = _bf(r1)
    x3 = _bf(r1 - x2.astype(F32))
    return x1, x2, x3


def _mm01_left(m01, x, terms=3):
    d = functools.partial(jnp.dot, preferred_element_type=F32)
    if terms == 2:
        x1 = _bf(x)
        return d(m01, x1) + d(m01, _bf(x - x1.astype(F32)))
    x1, x2, x3 = _split3(x)
    return d(m01, x1) + d(m01, x2) + d(m01, x3)


def _mm01_right(x, m01):
    return jnp.dot(_bf(x), m01, preferred_element_type=F32)


def _mm01_tn(x, m01):
    x1, x2, x3 = _split3(x)
    d = functools.partial(lax.dot_general, dimension_numbers=(((0,), (0,)), ((), ())), preferred_element_type=F32)
    return d(x1, m01) + d(x2, m01) + d(x3, m01)


def _seg_sum(x, ones_bd):
    blk = ones_bd.shape[0]
    n = x.shape[1] // blk
    parts = [_mm01_right(x[:, j * blk:(j + 1) * blk], ones_bd) for j in range(n)]
    return parts[0] if n == 1 else jnp.concatenate(parts, axis=1)


def _sigmoid(x):
    return 1.0 / (1.0 + jnp.exp(-x))


def _softplus(x):
    return jnp.maximum(x, 0.0) + jnp.log(1.0 + jnp.exp(-jnp.abs(x)))


def _stack_masked(x):
    group = lax.broadcasted_iota(jnp.int32, x.shape, 1) // CHUNK
    return jnp.concatenate([jnp.where(group == h, x, 0.0) for h in range(x.shape[1] // CHUNK)], axis=0)


def _side_masks(reverse, n):
    t = lax.broadcasted_iota(jnp.int32, (CHUNK, n * CHUNK), 0)
    s = lax.broadcasted_iota(jnp.int32, (CHUNK, n * CHUNK), 1) % CHUNK
    if reverse:
        return s > t, s >= t, s == t
    return s < t, s <= t, s == t


def _tri_incl(reverse):
    t = lax.broadcasted_iota(jnp.int32, (CHUNK, CHUNK), 0)
    s = lax.broadcasted_iota(jnp.int32, (CHUNK, CHUNK), 1)
    m = (s >= t) if reverse else (s <= t)
    return jnp.where(m, 1.0, 0.0).astype(BF16)


INVERSE_DOUBLINGS = CHUNK.bit_length() - 3


def _inverse_init(a_side, eye):
    n = -a_side
    return _mm(n, _stack_masked(n)), jnp.where(eye, 1.0, 0.0) + n


def _inverse_double(p, t):
    pt = _mm(jnp.concatenate([p, t], axis=0), _stack_masked(p))
    return pt[:CHUNK], t + pt[CHUNK:]


def _inverse_last(p, t):
    return t + _mm(t, _stack_masked(p))


def _norm_matmul_kernel(x_ref, g_ref, w_ref, o_ref, u_ref):
    @pl.when(pl.program_id(1) == 0)
    def _():
        x = x_ref[...]
        ms = jnp.mean(x * x, axis=-1, keepdims=True)
        u_ref[...] = _bf(x * lax.rsqrt(ms + RMS_EPS) * g_ref[...])

    o_ref[...] = jnp.dot(u_ref[...], w_ref[...], preferred_element_type=F32)


def _norm_matmul(x, gain, w, tm, tn):
    t, d = x.shape
    n = w.shape[1]
    return pl.pallas_call(
        _norm_matmul_kernel,
        grid=(t // tm, n // tn),
        in_specs=[pl.BlockSpec((tm, d), lambda i, j: (i, 0)),
                  pl.BlockSpec((1, d), lambda i, j: (0, 0)),
                  pl.BlockSpec((d, tn), lambda i, j: (0, j))],
        out_specs=[pl.BlockSpec((tm, tn), lambda i, j: (i, j)),
                   pl.BlockSpec((tm, d), lambda i, j: (i, 0))],
        out_shape=[jax.ShapeDtypeStruct((t, n), F32), jax.ShapeDtypeStruct((t, d), BF16)],
        compiler_params=_cparams(("parallel", "arbitrary")),
        name="in_proj",
    )(x, gain, w)


MXU_TILE = 256


HALO = 16


def _projection_pieces(u_ref, uprev_ref, unext_ref, w_ref, p_ref):
    u = jnp.concatenate([uprev_ref[...], u_ref[...], unext_ref[...]], axis=0)
    ncol = w_ref.shape[1]

    def piece(c0):
        c1 = min(c0 + MXU_TILE, ncol)
        p_ref[:, c0:c1] = jnp.dot(u, w_ref[:, c0:c1], preferred_element_type=F32)

    return [functools.partial(piece, c0) for c0 in range(0, ncol, MXU_TILE)]


def _halo_tile(p_ref, pos, tiles_per_seq):
    n = p_ref.shape[0]
    prev = jnp.where(pos == 0, 0.0, p_ref[0:HALO, :])
    nxt = jnp.where(pos == tiles_per_seq - 1, 0.0, p_ref[n - HALO:n, :])
    return jnp.concatenate([prev, p_ref[HALO:n - HALO, :], nxt], axis=0)


def _front_kernel(math, tiles_per_seq, n_params, n_out, *refs):
    u_ref, uprev_ref, unext_ref, w_ref = refs[:4]
    params = refs[4:4 + n_params]
    outs = refs[4 + n_params:4 + n_params + n_out]
    pa_ref, pb_ref = refs[4 + n_params + n_out:]
    i = pl.program_id(0)
    pos = (i + tiles_per_seq - 1) % tiles_per_seq

    @pl.when(i == 0)
    def _():
        pb_ref[...] = jnp.zeros_like(pb_ref)

    def step(p_new, p_old):
        pieces = _projection_pieces(u_ref, uprev_ref, unext_ref, w_ref, p_new)

        per_tick = -(-len(pieces) // (math.ticks + 1))

        def tick():
            for _ in range(min(per_tick, len(pieces))):
                pieces.pop(0)()

        tick()
        math(_halo_tile(p_old, pos, tiles_per_seq), params, outs, tick)
        assert not pieces

    @pl.when(i % 2 == 0)
    def _():
        step(pa_ref, pb_ref)

    @pl.when(i % 2 == 1)
    def _():
        step(pb_ref, pa_ref)


def _front(math, name, u, w, params, param_specs, out_specs, out_shapes, seq, ts):
    t, d = u.shape
    n = t // ts
    nhalo = t // HALO
    rpb = ts // HALO
    cur = lambda i: jnp.minimum(i, n - 1)
    shifted = lambda spec: pl.BlockSpec(spec.block_shape, lambda i, f=spec.index_map: f(jnp.maximum(i - 1, 0)))
    return pl.pallas_call(
        functools.partial(_front_kernel, math, seq // ts, len(params), len(out_specs)),
        grid=(n + 1,),
        in_specs=[pl.BlockSpec((ts, d), lambda i: (cur(i), 0)),
                  pl.BlockSpec((HALO, d), lambda i: (jnp.maximum(cur(i) * rpb - 1, 0), 0)),
                  pl.BlockSpec((HALO, d), lambda i: (jnp.minimum((cur(i) + 1) * rpb, nhalo - 1), 0)),
                  pl.BlockSpec(w.shape, lambda i: (0, 0), pipeline_mode=pl.Buffered(1))] + param_specs,
        out_specs=[shifted(s) for s in out_specs],
        out_shape=out_shapes,
        scratch_shapes=[pltpu.VMEM((ts + 2 * HALO, w.shape[1]), F32)] * 2,
        compiler_params=pltpu.CompilerParams(dimension_semantics=("arbitrary",), vmem_limit_bytes=FRONT_VMEM_LIMIT),
        name=name,
    )(u, u, u, w, *params)


def _rw_prep_math(ext, params, outs, tick):
    mu_ref, w0_ref, w2_ref, a0_ref, a2_ref, g2_ref, kk_ref, ka_ref, rk_ref, bd_ref = params
    v_o, g_o, bonus_o, rf_o, kf_o, bf_o, kkf_o, gcf_o, rb_o, kb_o, bb_o, kkb_o, gcb_o = outs
    n = ext.shape[0]
    ts = n - 2 * HALO
    w = RW_WIDTH

    def mixed(c0, c1):
        e = ext[:, c0:c1]
        body = slice(HALO, HALO + ts)
        mu = mu_ref[:, c0:c1]
        out = e[body] * (1.0 - mu) + (pltpu.roll(e, 1, axis=0)[body] + pltpu.roll(e, n - 1, axis=0)[body]) * (0.5 * mu)
        tick()
        return out

    lora = mixed(3 * w, RW_BLOCK)
    wlin = _mm(jnp.tanh(lora[:, 0:LANES]), w2_ref[...]) + w0_ref[...]
    alin = _mm(lora[:, LANES:2 * LANES], a2_ref[...]) + a0_ref[...]
    g = _mm(_sigmoid(lora[:, 2 * LANES:4 * LANES]), g2_ref[...])
    r = mixed(0, w)
    k = mixed(w, 2 * w)
    v = mixed(2 * w, 3 * w)
    lw = -RW_DECAY_SCALE * _sigmoid(wlin)
    tick()
    a = _sigmoid(alin)
    tick()

    bd = bd_ref[...]
    kscaled = k * kk_ref[...]
    kkn = kscaled * lax.rsqrt(_seg_sum(kscaled * kscaled, bd) + L2_EPS)
    tick()
    ka = ka_ref[...]
    a_f = a[:, 0:w]
    a_b = a[:, w:2 * w]
    k_f = k * (1.0 + (a_f - 1.0) * ka)
    k_b = k * (1.0 + (a_b - 1.0) * ka)
    tick()
    bonus = _seg_sum(r * (0.5 * (k_f + k_b)) * rk_ref[...], bd) * v

    v_o[...] = _bf(v)
    g_o[...] = g
    bonus_o[...] = bonus
    tick()

    rt = lax.broadcasted_iota(jnp.int32, (ts, ts), 0)
    rs = lax.broadcasted_iota(jnp.int32, (ts, ts), 1)
    same_chunk = (rt // CHUNK) == (rs // CHUNK)
    for d, (k_d, a_d, r_o, k_o, b_o, kk_o, gc_o) in enumerate(((k_f, a_f, rf_o, kf_o, bf_o, kkf_o, gcf_o),
                                                                (k_b, a_b, rb_o, kb_o, bb_o, kkb_o, gcb_o))):
        lw_d = lw[:, d * w:(d + 1) * w]
        before = (rs <= rt) if d == 0 else (rs >= rt)
        cum = _mm01_left(jnp.where(same_chunk & before, 1.0, 0.0).astype(BF16), lw_d, terms=2)
        tick()
        g_inv = jnp.exp(-cum)
        r_o[...] = _bf(r * jnp.exp(cum))
        tick()
        k_o[...] = _bf(k_d * g_inv)
        b_o[...] = _bf(kkn * a_d * g_inv)
        tick()
        kk_o[...] = _bf(kkn * jnp.exp(cum - lw_d))
        for c in range(ts // CHUNK):
            last = c * CHUNK + (CHUNK - 1 if d == 0 else 0)
            gc_o[c] = jnp.exp(cum[last:last + 1, :])
        tick()


_rw_prep_math.ticks = 17


def _rw_front(u, w_rw, seq, ts, mu, w0, w2, a0, a2, g2, k_k, k_a, r_k, bd64):
    t = u.shape[0]
    params = [mu, w0, w2, a0, a2, g2, k_k, k_a, r_k, bd64]
    cpt = ts // CHUNK
    big = pl.BlockSpec((ts, RW_WIDTH), lambda i: (i, 0))
    tot = pl.BlockSpec((cpt, 1, RW_WIDTH), lambda i: (i, 0, 0))
    wide = lambda dt: jax.ShapeDtypeStruct((t, RW_WIDTH), dt)
    tots = jax.ShapeDtypeStruct((t // CHUNK, 1, RW_WIDTH), F32)
    direction = [wide(BF16)] * 4 + [tots]
    return _front(_rw_prep_math, "rw_front", u, w_rw, params,
                  [pl.BlockSpec(a.shape, lambda i: (0, 0)) for a in params],
                  [big] * 3 + ([big] * 4 + [tot]) * 2,
                  [wide(BF16), wide(F32), wide(F32)] + direction * 2, seq, ts)


RW_GROUP = 4
RW_GROUP_W = RW_GROUP * RW_HEAD_DIM


def _rw_scan_kernel(reverse, v_ref, r_ref, k_ref, b_ref, kk_ref, gc_ref, *rest):
    acc_ref = rest[0] if len(rest) == 3 else None
    y_ref, s_ref = rest[-2:]
    @pl.when(pl.program_id(0) == 0)
    def _():
        s_ref[...] = jnp.zeros_like(s_ref)

    gw = RW_GROUP_W
    nb, rows, width = r_ref.shape
    nblk = width // gw
    nch = rows // CHUNK
    strict, incl, eye = _side_masks(reverse, RW_GROUP)
    ri = lax.broadcasted_iota(jnp.int32, (gw, gw), 0) // CHUNK
    ci = lax.broadcasted_iota(jnp.int32, (gw, gw), 1) // CHUNK
    bd_mask = ri == ci
    order = list(range(nch - 1, -1, -1) if reverse else range(nch))
    lanes = [(bi, blk) for bi in range(nb) for blk in range(nblk)]
    chains = [(bi, blk, c) for c in order for (bi, blk) in lanes]

    def tile(ref, bi, blk, c):
        return ref[bi, c * CHUNK:(c + 1) * CHUNK, blk * gw:(blk + 1) * gw]

    r_t = {ch: tile(r_ref, *ch) for ch in chains}
    k_t = {ch: tile(k_ref, *ch) for ch in chains}
    b_t = {ch: tile(b_ref, *ch) for ch in chains}
    kk_t = {ch: tile(kk_ref, *ch) for ch in chains}
    vv = {ch: tile(v_ref, *ch) for ch in chains}

    a_v, a_kb, a_rb, pp, tt = {}, {}, {}, {}, {}
    for ch in chains:
        lhs = jnp.concatenate([kk_t[ch], r_t[ch]], axis=0)
        a_k = _mm_nt(lhs, _stack_masked(k_t[ch]))
        a_b = _mm_nt(lhs, _stack_masked(b_t[ch]))
        a_v[ch] = jnp.concatenate([jnp.where(strict, a_k[:CHUNK], 0.0), jnp.where(incl, a_k[CHUNK:], 0.0)], axis=0)
        a_rb[ch] = jnp.where(incl, a_b[CHUNK:], 0.0)
        a_kb[ch] = jnp.where(strict, a_b[:CHUNK], 0.0)
    for ch in chains:
        pp[ch], tt[ch] = _inverse_init(a_kb[ch], eye)
    for _ in range(INVERSE_DOUBLINGS):
        for ch in chains:
            pp[ch], tt[ch] = _inverse_double(pp[ch], tt[ch])
    for ch in chains:
        tt[ch] = _inverse_last(pp[ch], tt[ch])

    w_mat, u0, y0 = {}, {}, {}
    for ch in chains:
        av = _mm(a_v[ch], _stack_masked(vv[ch]))
        y0[ch] = av[CHUNK:]
        u0[ch] = av[:CHUNK]
        w_mat[ch] = _mm(tt[ch], _stack_masked(kk_t[ch]))
    for ch in chains:
        u0[ch] = _mm(tt[ch], _stack_masked(u0[ch]))

    state = {(bi, blk): s_ref[bi * nblk + blk] for (bi, blk) in lanes}
    for c in order:
        ws = {ln: _mm_nt(jnp.concatenate([_bf(w_mat[(*ln, c)]), r_t[(*ln, c)]], axis=0), state[ln]) for ln in lanes}
        u = {ln: ws[ln][:CHUNK] + u0[(*ln, c)] for ln in lanes}
        ds = {ln: _mm_tn(jnp.concatenate([vv[(*ln, c)], _bf(-u[ln])], axis=0),
                         jnp.concatenate([k_t[(*ln, c)], b_t[(*ln, c)]], axis=0)) for ln in lanes}
        for (bi, blk) in lanes:
            ln = (bi, blk)
            state[ln] = (state[ln] + jnp.where(bd_mask, ds[ln], 0.0)) * gc_ref[bi, c, :, blk * gw:(blk + 1) * gw]
            y = ws[ln][CHUNK:] + y0[(bi, blk, c)] - _mm(a_rb[(bi, blk, c)], _stack_masked(u[ln]))
            where = (bi, slice(c * CHUNK, (c + 1) * CHUNK), slice(blk * gw, (blk + 1) * gw))
            y_ref[where] = y if acc_ref is None else y + acc_ref[where]
    for (bi, blk) in lanes:
        s_ref[bi * nblk + blk] = state[(bi, blk)]


def _rw_scan(v, r_t, k_t, b_t, kk_t, gc, batch, tt, reverse, acc=None):
    t = v.shape[0]
    seq = t // batch
    nt = seq // tt
    cpt = tt // CHUNK
    rowi = (lambda c: nt - 1 - c) if reverse else (lambda c: c)
    spec = pl.BlockSpec((batch, tt, RW_WIDTH), lambda c: (0, rowi(c), 0))
    seqs = lambda a: a.reshape(batch, seq, RW_WIDTH)
    extra = [] if acc is None else [seqs(acc)]
    y = pl.pallas_call(
        functools.partial(_rw_scan_kernel, reverse),
        grid=(nt,),
        in_specs=([spec] * 5 + [pl.BlockSpec((batch, cpt, 1, RW_WIDTH), lambda c: (0, rowi(c), 0, 0))]
                  + [spec] * len(extra)),
        out_specs=spec,
        out_shape=jax.ShapeDtypeStruct((batch, seq, RW_WIDTH), F32),
        scratch_shapes=[pltpu.VMEM((batch * RW_WIDTH // RW_GROUP_W, RW_GROUP_W, RW_GROUP_W), F32)],
        compiler_params=_cparams(("arbitrary",)),
        name="rw_scan_bwd" if reverse else "rw_scan_fwd",
    )(seqs(v), seqs(r_t), seqs(k_t), seqs(b_t), seqs(kk_t), gc.reshape(batch, seq // CHUNK, 1, RW_WIDTH), *extra)
    return y.reshape(t, RW_WIDTH)


def _rw_post_math(y, bonus, g, gn_w, gn_b, bd):
    inv_n = 1.0 / RW_HEAD_DIM
    mean = _seg_sum(y, bd) * inv_n
    yc = y - mean
    var = _seg_sum(yc * yc, bd) * inv_n
    yn = yc * lax.rsqrt(var + RW_GN_EPS) * gn_w + gn_b
    return _bf((yn + bonus) * g)


def _gdn_prep_math(ext_all, params, outs, tick):
    cw_ref, alog_ref, dtb_ref, ones_ref = params
    q_o, k_o, v_o, gates_o = outs
    n = ext_all.shape[0]
    ts = n - 2 * HALO
    half = GDN_CONV // 2

    def conv_silu(c0, c1):
        e = ext_all[:, c0:c1]
        acc = None
        for j in range(GDN_CONV):
            shift = (half - j) % n
            xs = e if shift == 0 else pltpu.roll(e, shift, axis=0)
            term = cw_ref[j:j + 1, c0:c1] * xs[HALO:HALO + ts, :]
            acc = term if acc is None else acc + term
        tick()
        out = acc * _sigmoid(acc)
        tick()
        return out

    ones = ones_ref[...]
    qw = GDN_QK_WIDTH
    q = conv_silu(0, qw)
    q_o[...] = q * lax.rsqrt(_seg_sum(q * q, ones) + L2_EPS) * (GDN_HEAD_DIM ** -0.5)
    k = conv_silu(qw, 2 * qw)
    k_o[...] = k * lax.rsqrt(_seg_sum(k * k, ones) + L2_EPS)
    v_o[:, 0:qw] = conv_silu(2 * qw, 3 * qw)
    v_o[:, qw:2 * qw] = conv_silu(3 * qw, 4 * qw)

    gx = ext_all[HALO:HALO + ts, GDN_CONV_CH:]
    lane = lax.broadcasted_iota(jnp.int32, gx.shape, 1)
    log_decay = -jnp.exp(alog_ref[...]) * _softplus(gx + dtb_ref[...])
    gates_o[...] = jnp.where(lane < 2 * GDN_V_HEADS, log_decay, _sigmoid(gx))


_gdn_prep_math.ticks = 8


def _gdn_front(u, w_gdn, seq, ts, conv_w, a_log, dt_bias, ones128):
    t = u.shape[0]
    params = [conv_w, a_log, dt_bias, ones128]
    widths = (GDN_QK_WIDTH, GDN_QK_WIDTH, GDN_V_WIDTH, LANES)
    return _front(_gdn_prep_math, "gdn_front", u, w_gdn, params,
                  [pl.BlockSpec(a.shape, lambda i: (0, 0)) for a in params],
                  [pl.BlockSpec((ts, wd), lambda i: (i, 0)) for wd in widths],
                  [jax.ShapeDtypeStruct((t, wd), F32) for wd in widths], seq, ts)


def _gdn_scan_kernel(reverse, g_off, b_off, q_ref, k_ref, v_ref, gates_ref, *rest):
    acc_ref = rest[0] if len(rest) == 3 else None
    o_ref, s_ref = rest[-2:]

    @pl.when(pl.program_id(0) == 0)
    def _():
        s_ref[...] = jnp.zeros_like(s_ref)

    nb = q_ref.shape[0]
    nch = q_ref.shape[1] // CHUNK
    strict, incl, _ = _side_masks(reverse, 2)
    eye4 = _side_masks(reverse, 4)[2]
    tri = _tri_incl(reverse)
    s_i = lax.broadcasted_iota(jnp.int32, (CHUNK, 2 * CHUNK), 0)
    t_i = lax.broadcasted_iota(jnp.int32, (CHUNK, 2 * CHUNK), 1) % CHUNK
    tri_t2 = jnp.where((s_i >= t_i) if reverse else (s_i <= t_i), 1.0, 0.0).astype(BF16)
    first = lax.broadcasted_iota(jnp.int32, (CHUNK, 2 * CHUNK), 1) < CHUNK
    zero = jnp.zeros((CHUNK, LANES), F32)
    zero_s = jnp.zeros((LANES, LANES), F32)
    last = 0 if reverse else CHUNK - 1
    order = list(range(nch - 1, -1, -1) if reverse else range(nch))
    slots = [(bi, c) for c in order for bi in range(nb)]
    chains = [(bi, j, c) for (bi, c) in slots for j in range(GDN_QK_HEADS)]

    def rows(c):
        return slice(c * CHUNK, (c + 1) * CHUNK)

    gates, ccol, crow = {}, {}, {}
    for sl in slots:
        gates[sl] = gates_ref[sl[0], rows(sl[1]), :]
        ccol[sl] = _mm01_left(tri, gates[sl])
        crow[sl] = _mm01_tn(gates[sl], tri_t2)

    q, k, qk_pair, a_pair = {}, {}, {}, {}
    for ch in chains:
        bi, j, c = ch
        sl = (bi, c)
        i0, i1 = g_off + 2 * j, g_off + 2 * j + 1
        q[ch] = q_ref[bi, rows(c), j * LANES:(j + 1) * LANES]
        k[ch] = k_ref[bi, rows(c), j * LANES:(j + 1) * LANES]
        diff = (jnp.where(first, ccol[sl][:, i0:i0 + 1], ccol[sl][:, i1:i1 + 1])
                - jnp.where(first, crow[sl][i0:i0 + 1, :], crow[sl][i1:i1 + 1, :]))
        decay = jnp.where(incl, jnp.exp(jnp.minimum(diff, 0.0)), 0.0)
        beta_pair = jnp.where(first, gates[sl][:, b_off + 2 * j:b_off + 2 * j + 1],
                              gates[sl][:, b_off + 2 * j + 1:b_off + 2 * j + 2])
        kdup = jnp.concatenate([k[ch], k[ch]], axis=0)
        kq = _mm_nt(jnp.concatenate([k[ch], q[ch]], axis=0), kdup)
        qk_pair[ch] = kq[CHUNK:] * decay
        a_pair[ch] = jnp.where(strict, beta_pair * kq[:CHUNK] * decay, 0.0)

    quads = [(bi, jj, c) for (bi, c) in slots for jj in range(GDN_QK_HEADS // 2)]
    pp, tq = {}, {}
    for (bi, jj, c) in quads:
        pp[(bi, jj, c)], tq[(bi, jj, c)] = _inverse_init(
            jnp.concatenate([a_pair[(bi, 2 * jj, c)], a_pair[(bi, 2 * jj + 1, c)]], axis=1), eye4)
    for _ in range(INVERSE_DOUBLINGS):
        for qd in quads:
            pp[qd], tq[qd] = _inverse_double(pp[qd], tq[qd])
    tt = {}
    for (bi, jj, c) in quads:
        t4 = _inverse_last(pp[(bi, jj, c)], tq[(bi, jj, c)])
        tt[(bi, 2 * jj, c)] = t4[:, :LANES]
        tt[(bi, 2 * jj + 1, c)] = t4[:, LANES:]

    sol, wq = {}, {}
    for ch in chains:
        bi, j, c = ch
        parts = []
        for e in range(2):
            h = 2 * j + e
            beta = gates[(bi, c)][:, b_off + h:b_off + h + 1]
            eg = jnp.exp(ccol[(bi, c)][:, g_off + h:g_off + h + 1])
            v_e = v_ref[bi, rows(c), h * LANES:(h + 1) * LANES]
            blocks = [zero] * 4
            blocks[2 * e] = v_e * beta
            blocks[2 * e + 1] = k[ch] * (beta * eg)
            parts.append(jnp.concatenate(blocks, axis=1))
            wq[(bi, j, c, e)] = q[ch] * eg
        sol[ch] = _mm(tt[ch], jnp.concatenate(parts, axis=0))

    pairs = [(bi, j) for bi in range(nb) for j in range(GDN_QK_HEADS)]
    heads = [(bi, j, e) for (bi, j) in pairs for e in range(2)]
    state = {(bi, j, e): s_ref[(bi * GDN_QK_HEADS + j) * 2 + e] for (bi, j, e) in heads}
    for c in order:
        ws = {}
        for (bi, j) in pairs:
            so = sol[(bi, j, c)]
            lhs = jnp.concatenate([jnp.concatenate([so[:, LANES:2 * LANES], so[:, 3 * LANES:]], axis=1),
                                   jnp.concatenate([wq[(bi, j, c, 0)], wq[(bi, j, c, 1)]], axis=1)], axis=0)
            s_bd = jnp.concatenate([jnp.concatenate([state[(bi, j, 0)], zero_s], axis=1),
                                    jnp.concatenate([zero_s, state[(bi, j, 1)]], axis=1)], axis=0)
            ws[(bi, j)] = _mm(lhs, s_bd)
        vn = {(bi, j, e): (sol[(bi, j, c)][:, 2 * e * LANES:(2 * e + 1) * LANES]
                           - ws[(bi, j)][:CHUNK, e * LANES:(e + 1) * LANES]) for (bi, j, e) in heads}
        for (bi, j, e) in heads:
            cc = ccol[(bi, c)][:, g_off + 2 * j + e:g_off + 2 * j + e + 1]
            tot = cc[last:last + 1, :]
            kg = k[(bi, j, c)] * jnp.exp(tot - cc)
            state[(bi, j, e)] = state[(bi, j, e)] * jnp.exp(tot) + _mm_tn(kg, vn[(bi, j, e)])
        for (bi, j) in pairs:
            vn_bd = jnp.concatenate([jnp.concatenate([vn[(bi, j, 0)], zero], axis=1),
                                     jnp.concatenate([zero, vn[(bi, j, 1)]], axis=1)], axis=0)
            o_intra = _mm(qk_pair[(bi, j, c)], vn_bd)
            where = (bi, rows(c), slice(2 * j * LANES, (2 * j + 2) * LANES))
            o = ws[(bi, j)][CHUNK:] + o_intra
            o_ref[where] = o if acc_ref is None else o + acc_ref[where]
    for (bi, j, e) in heads:
        s_ref[(bi * GDN_QK_HEADS + j) * 2 + e] = state[(bi, j, e)]


def _gdn_scan(q, k, v, gates, batch, tt, reverse, acc=None):
    t = q.shape[0]
    seq = t // batch
    nt = seq // tt
    rowi = (lambda c: nt - 1 - c) if reverse else (lambda c: c)
    g_off = GDN_V_HEADS if reverse else 0
    b_off = 2 * GDN_V_HEADS + g_off
    spec = lambda width: pl.BlockSpec((batch, tt, width), lambda c: (0, rowi(c), 0))
    seqs = lambda a: a.reshape(batch, seq, a.shape[-1])
    extra = [] if acc is None else [seqs(acc)]
    o = pl.pallas_call(
        functools.partial(_gdn_scan_kernel, reverse, g_off, b_off),
        grid=(nt,),
        in_specs=[spec(GDN_QK_WIDTH), spec(GDN_QK_WIDTH), spec(GDN_V_WIDTH), spec(LANES)]
        + [spec(GDN_V_WIDTH)] * len(extra),
        out_specs=spec(GDN_V_WIDTH),
        out_shape=jax.ShapeDtypeStruct((batch, seq, GDN_V_WIDTH), F32),
        scratch_shapes=[pltpu.VMEM((batch * GDN_V_HEADS, LANES, LANES), F32)],
        compiler_params=_cparams(("arbitrary",)),
        name="gdn_scan_bwd" if reverse else "gdn_scan_fwd",
    )(seqs(q), seqs(k), seqs(v), seqs(gates), *extra)
    return o.reshape(t, GDN_V_WIDTH)


def _gdn_post_math(o, z, norm_w, ones):
    ms = _seg_sum(o * o, ones) * (1.0 / GDN_HEAD_DIM)
    return _bf(o * lax.rsqrt(ms + GDN_NORM_EPS) * norm_w * (z * _sigmoid(z)))


MERGE_SUB = 128


def _mix_out_kernel(y_ref, bonus_ref, g_ref, gnw_ref, gnb_ref, bd_ref, o_ref, z_ref, nw_ref, ones_ref,
                    gr_ref, gg_ref, pa_ref, pb_ref, x_ref, wo_ref, gpost_ref, gnext_ref, h_ref, u_ref):
    parts = []
    for s in range(y_ref.shape[0] // MERGE_SUB):
        rows = slice(s * MERGE_SUB, (s + 1) * MERGE_SUB)
        y_rw = _rw_post_math(y_ref[rows, :], bonus_ref[rows, :], g_ref[rows, :],
                             gnw_ref[...], gnb_ref[...], bd_ref[...])
        y_gdn = _gdn_post_math(o_ref[rows, :], z_ref[rows, :], nw_ref[...], ones_ref[...])
        a = jnp.dot(y_rw, pa_ref[...], preferred_element_type=F32)
        b = jnp.dot(y_gdn, pb_ref[...], preferred_element_type=F32)
        parts.append(_bf(_sigmoid(gr_ref[rows, :]) * a + _sigmoid(gg_ref[rows, :]) * b))
    m = jnp.concatenate(parts, axis=0)
    yo = jnp.dot(m, wo_ref[...], preferred_element_type=F32)
    ms = jnp.mean(yo * yo, axis=-1, keepdims=True)
    h = x_ref[...] + yo * lax.rsqrt(ms + RMS_EPS) * gpost_ref[...]
    h_ref[...] = h
    ms2 = jnp.mean(h * h, axis=-1, keepdims=True)
    u_ref[...] = _bf(h * lax.rsqrt(ms2 + RMS_EPS) * gnext_ref[...])


def _mix_out(y, bonus, g, gn_w, gn_b, bd64, o, p, norm_w, ones128, w_a, w_b, x, w_out, gain, gain_next, tm):
    t, d = x.shape
    wide = pl.BlockSpec((tm, RW_WIDTH), lambda i: (i, 0))
    row = lambda width: pl.BlockSpec((1, width), lambda i: (0, 0))
    const = lambda a: pl.BlockSpec(a.shape, lambda i: (0, 0))
    weight = lambda a: pl.BlockSpec(a.shape, lambda i: (0, 0), pipeline_mode=pl.Buffered(1))
    pcol = lambda width, off: pl.BlockSpec((tm, width), lambda i: (i, off // width))
    tile = pl.BlockSpec((tm, d), lambda i: (i, 0))
    return pl.pallas_call(
        _mix_out_kernel,
        grid=(t // tm,),
        in_specs=[wide, wide, wide, row(RW_WIDTH), row(RW_WIDTH), const(bd64),
                  wide, pcol(GDN_V_WIDTH, P_GZ), row(GDN_V_WIDTH), const(ones128),
                  pcol(d, P_MG_RW), pcol(d, P_MG_GDN), weight(w_a), weight(w_b),
                  tile, weight(w_out), row(d), row(d)],
        out_specs=[tile, tile],
        out_shape=[jax.ShapeDtypeStruct((t, d), F32), jax.ShapeDtypeStruct((t, d), BF16)],
        compiler_params=pltpu.CompilerParams(dimension_semantics=("parallel",), vmem_limit_bytes=FRONT_VMEM_LIMIT),
        name="mix_out",
    )(y, bonus, g, gn_w, gn_b, bd64, o, p, norm_w, ones128, p, p, w_a, w_b, x, w_out, gain, gain_next)


def _ffn_kernel(h_ref, u_ref, wg_ref, wu_ref, wd_ref, gpost_ref, o_ref):
    j = pl.program_id(1)

    @pl.when(j == 0)
    def _():
        o_ref[...] = jnp.zeros_like(o_ref)

    u = u_ref[...]
    gate = jnp.dot(u, wg_ref[...], preferred_element_type=F32)
    up = jnp.dot(u, wu_ref[...], preferred_element_type=F32)
    f = _bf(gate * _sigmoid(gate) * up)
    o_ref[...] += jnp.dot(f, wd_ref[...], preferred_element_type=F32)

    @pl.when(j == pl.num_programs(1) - 1)
    def _():
        y = o_ref[...]
        ms = jnp.mean(y * y, axis=-1, keepdims=True)
        o_ref[...] = h_ref[...] + y * lax.rsqrt(ms + RMS_EPS) * gpost_ref[...]


def _ffn(h, u, wg, wu, wd, gpost, tm, tf):
    t, d = h.shape
    f = wg.shape[1]
    assert t % tm == 0 and f % tf == 0, (t, tm, f, tf)
    return pl.pallas_call(
        _ffn_kernel,
        grid=(t // tm, f // tf),
        in_specs=[pl.BlockSpec((tm, d), lambda i, j: (i, 0)),
                  pl.BlockSpec((tm, d), lambda i, j: (i, 0)),
                  pl.BlockSpec((d, tf), lambda i, j: (0, j)),
                  pl.BlockSpec((d, tf), lambda i, j: (0, j)),
                  pl.BlockSpec((tf, d), lambda i, j: (j, 0)),
                  pl.BlockSpec((1, d), lambda i, j: (0, 0))],
        out_specs=pl.BlockSpec((tm, d), lambda i, j: (i, 0)),
        out_shape=jax.ShapeDtypeStruct((t, d), F32),
        compiler_params=_cparams(("parallel", "arbitrary")),
        name="ffn",
    )(h, u, wg, wu, wd, gpost)


def _pack_w_in(w_in):
    d = w_in.shape[0]
    rw = w_in[:, :RW_IN]
    gdn = w_in[:, RW_IN:RW_IN + GDN_IN]
    gates = w_in[:, RW_IN + GDN_IN:]
    z = lambda n: jnp.zeros((d, n), w_in.dtype)
    cat = lambda cols: _bf(jnp.concatenate(cols, axis=1))
    w_rw = cat([rw, z(RW_BLOCK - RW_IN)])
    w_gdn = cat([gdn[:, :GDN_CONV_CH], gdn[:, GDN_CONV_CH + GDN_V_WIDTH:], z(LANES - 4 * GDN_V_HEADS)])
    w_rest = cat([gates, gdn[:, GDN_CONV_CH:GDN_CONV_CH + GDN_V_WIDTH]])
    return w_rw, w_gdn, w_rest


def _block_diag2(a, b):
    z = jnp.zeros_like(a)
    return jnp.concatenate([jnp.concatenate([a, z], axis=1), jnp.concatenate([z, b], axis=1)], axis=0)


def _group_ones(group):
    i = jnp.arange(MXU_TILE)
    return (i[:, None] // group == i[None, :] // group).astype(BF16)


def _forward(x, norm_pre_mix, w_in, rw_shift_mu, rw_w0_f, rw_w2_f, rw_w0_b, rw_w2_b, rw_a0_f, rw_a2_f, rw_a0_b,
             rw_a2_b, rw_g2, rw_k_k, rw_k_a, rw_r_k, rw_gn_w, rw_gn_b, gdn_conv_w, gdn_a_log_f, gdn_dt_bias_f,
             gdn_a_log_b, gdn_dt_bias_b, gdn_norm_w, w_branch_rw, w_branch_gdn, w_out, norm_post_mix, norm_pre_ffn,
             w_ffn_gate, w_ffn_up, w_ffn_down, norm_post_ffn, *, tiles):
    batch, seq, d = x.shape
    t = batch * seq
    xf = x.reshape(t, d)
    row = lambda a: a.reshape(1, -1).astype(F32)
    bd64 = _group_ones(RW_HEAD_DIM)
    ones128 = _group_ones(GDN_HEAD_DIM)

    w_rw, w_gdn, w_rest = _pack_w_in(w_in)
    p, u_in = _norm_matmul(xf, row(norm_pre_mix), w_rest, tiles["in_tm"], tiles["in_tn"])

    mu = jnp.pad(row(rw_shift_mu), ((0, 0), (0, RW_BLOCK - RW_IN)))
    w0 = jnp.concatenate([row(rw_w0_f), row(rw_w0_b)], axis=1)
    a0 = jnp.concatenate([row(rw_a0_f), row(rw_a0_b)], axis=1)
    w2 = _bf(_block_diag2(rw_w2_f, rw_w2_b))
    a2 = _bf(_block_diag2(rw_a2_f, rw_a2_b))
    g2 = _bf(jnp.pad(rw_g2, ((0, 2 * LANES - RW_GATE_LORA), (0, 0))))
    v, g, bonus, *scaled = _rw_front(
        u_in, w_rw, seq, tiles["prep_ts"], mu, w0, w2, a0, a2, g2, row(rw_k_k), row(rw_k_a), row(rw_r_k), bd64)
    y_f = _rw_scan(v, *scaled[:5], batch, tiles["rw_tt"], False)
    y = _rw_scan(v, *scaled[5:], batch, tiles["rw_tt"], True, acc=y_f)

    conv_w = jnp.pad(gdn_conv_w.astype(F32), ((0, SUBLANES - GDN_CONV), (0, 0)))
    nv = GDN_V_HEADS
    a_log = jnp.pad(jnp.concatenate([gdn_a_log_f, gdn_a_log_b]).reshape(1, -1), ((0, 0), (0, LANES - 2 * nv)))
    dt_bias = jnp.pad(jnp.concatenate([gdn_dt_bias_f, gdn_dt_bias_b]).reshape(1, -1), ((0, 0), (0, LANES - 2 * nv)))
    q, k, vg, gates = _gdn_front(u_in, w_gdn, seq, tiles["prep_ts"], conv_w, a_log.astype(F32),
                                 dt_bias.astype(F32), ones128)
    o_f = _gdn_scan(q, k, vg, gates, batch, tiles["gdn_tt"], False)
    o = _gdn_scan(q, k, vg, gates, batch, tiles["gdn_tt"], True, acc=o_f)
    nw = jnp.tile(row(gdn_norm_w), (1, GDN_V_HEADS))

    h1, u_ffn = _mix_out(y, bonus, g, row(rw_gn_w), row(rw_gn_b), bd64, o, p, nw, ones128,
                         _bf(w_branch_rw), _bf(w_branch_gdn), xf, _bf(w_out), row(norm_post_mix),
                         row(norm_pre_ffn), tiles["mix_tm"])
    h2 = _ffn(h1, u_ffn, _bf(w_ffn_gate), _bf(w_ffn_up), _bf(w_ffn_down), row(norm_post_ffn),
              tiles["ffn_tm"], tiles["ffn_tf"])
    return h2.reshape(batch, seq, d)


def _tiles(seq):
    pick = lambda want: min(want, seq)
    return dict(in_tm=pick(1024), in_tn=1024, prep_ts=pick(256), rw_tt=pick(128), gdn_tt=pick(128),
                mix_tm=pick(256), ffn_tm=pick(512), ffn_tf=512)


def kernel(x, norm_pre_mix, w_in, rw_shift_mu, rw_w0_f, rw_w2_f, rw_w0_b, rw_w2_b, rw_a0_f, rw_a2_f, rw_a0_b, rw_a2_b, rw_g2, rw_k_k, rw_k_a, rw_r_k, rw_gn_w, rw_gn_b, gdn_conv_w, gdn_a_log_f, gdn_dt_bias_f, gdn_a_log_b, gdn_dt_bias_b, gdn_norm_w, w_branch_rw, w_branch_gdn, w_out, norm_post_mix, norm_pre_ffn, w_ffn_gate, w_ffn_up, w_ffn_down, norm_post_ffn):
    args = [a[0] for a in (norm_pre_mix, w_in, rw_shift_mu, rw_w0_f, rw_w2_f, rw_w0_b, rw_w2_b, rw_a0_f, rw_a2_f,
                           rw_a0_b, rw_a2_b, rw_g2, rw_k_k, rw_k_a, rw_r_k, rw_gn_w, rw_gn_b, gdn_conv_w,
                           gdn_a_log_f, gdn_dt_bias_f, gdn_a_log_b, gdn_dt_bias_b, gdn_norm_w, w_branch_rw,
                           w_branch_gdn, w_out, norm_post_mix, norm_pre_ffn, w_ffn_gate, w_ffn_up, w_ffn_down,
                           norm_post_ffn)]
    assert norm_pre_mix.shape[0] == 1, "one layer"
    return _forward(x, *args, tiles=_tiles(x.shape[1]))
```

```python
import functools

import jax
import jax.numpy as jnp
from jax import lax
from jax.experimental import pallas as pl
from jax.experimental.pallas import tpu as pltpu

F32 = jnp.float32
BF16 = jnp.bfloat16

D_MODEL = 2048
RMS_EPS = 1e-6

RW_HEADS = 16
RW_HEAD_DIM = 64
RW_WIDTH = 1024
RW_LORA = 64
RW_GATE_LORA = 160
RW_GN_EPS = 64e-5
RW_IN = 3488
RW_DECAY_SCALE = 0.6065306597126334

GDN_QK_HEADS = 4
GDN_V_HEADS = 8
GDN_HEAD_DIM = 128
GDN_QK_WIDTH = 512
GDN_V_WIDTH = 1024
GDN_CONV_CH = 2048
GDN_CONV = 5
GDN_NORM_EPS = 1e-6
GDN_IN = 3104
L2_EPS = 1e-6

FFN_HIDDEN = 5632

CHUNK = 64
LANES = 128
SUBLANES = 8

RW_BLOCK = 3584
GDN_BLOCK = 2176
P_MG_RW = 0
P_MG_GDN = 2048
P_GZ = 4096
P_WIDTH = 5120

VMEM_LIMIT = 48 * 1024 * 1024
FRONT_VMEM_LIMIT = 56 * 1024 * 1024


def _cparams(sem):
    return pltpu.CompilerParams(dimension_semantics=sem, vmem_limit_bytes=VMEM_LIMIT)


def _bf(x):
    return x.astype(BF16)


def _mm(a, b):
    return jnp.dot(_bf(a), _bf(b), preferred_element_type=F32)


def _mm_nt(a, b):
    return lax.dot_general(_bf(a), _bf(b), (((1,), (1,)), ((), ())), preferred_element_type=F32)


def _mm_tn(a, b):
    return lax.dot_general(_bf(a), _bf(b), (((0,), (0,)), ((), ())), preferred_element_type=F32)


def _split3(x):
    x1 = _bf(x)
    r1 = x - x1.astype(F32)
    x2 = _bf(r1)
    x3 = _bf(r1 - x2.astype(F32))
    return x1, x2, x3


def _mm01_left(m01, x, terms=3):
    d = functools.partial(jnp.dot, preferred_element_type=F32)
    if terms == 2:
        x1 = _bf(x)
        return d(m01, x1) + d(m01, _bf(x - x1.astype(F32)))
    x1, x2, x3 = _split3(x)
    return d(m01, x1) + d(m01, x2) + d(m01, x3)


def _mm01_right(x, m01):
    return jnp.dot(_bf(x), m01, preferred_element_type=F32)


def _mm01_tn(x, m01):
    x1, x2, x3 = _split3(x)
    d = functools.partial(lax.dot_general, dimension_numbers=(((0,), (0,)), ((), ())), preferred_element_type=F32)
    return d(x1, m01) + d(x2, m01) + d(x3, m01)


def _seg_sum(x, ones_bd):
    blk = ones_bd.shape[0]
    n = x.shape[1] // blk
    parts = [_mm01_right(x[:, j * blk:(j + 1) * blk], ones_bd) for j in range(n)]
    return parts[0] if n == 1 else jnp.concatenate(parts, axis=1)


def _sigmoid(x):
    return 1.0 / (1.0 + jnp.exp(-x))


def _softplus(x):
    return jnp.maximum(x, 0.0) + jnp.log(1.0 + jnp.exp(-jnp.abs(x)))


def _stack_masked(x):
    group = lax.broadcasted_iota(jnp.int32, x.shape, 1) // CHUNK
    return jnp.concatenate([jnp.where(group == h, x, 0.0) for h in range(x.shape[1] // CHUNK)], axis=0)


def _side_masks(reverse, n):
    t = lax.broadcasted_iota(jnp.int32, (CHUNK, n * CHUNK), 0)
    s = lax.broadcasted_iota(jnp.int32, (CHUNK, n * CHUNK), 1) % CHUNK
    if reverse:
        return s > t, s >= t, s == t
    return s < t, s <= t, s == t


def _tri_incl(reverse):
    t = lax.broadcasted_iota(jnp.int32, (CHUNK, CHUNK), 0)
    s = lax.broadcasted_iota(jnp.int32, (CHUNK, CHUNK), 1)
    m = (s >= t) if reverse else (s <= t)
    return jnp.where(m, 1.0, 0.0).astype(BF16)


INVERSE_DOUBLINGS = CHUNK.bit_length() - 3


def _inverse_init(a_side, eye):
    n = -a_side
    return _mm(n, _stack_masked(n)), jnp.where(eye, 1.0, 0.0) + n


def _inverse_double(p, t):
    pt = _mm(jnp.concatenate([p, t], axis=0), _stack_masked(p))
    return pt[:CHUNK], t + pt[CHUNK:]


def _inverse_last(p, t):
    return t + _mm(t, _stack_masked(p))


def _norm_matmul_kernel(x_ref, g_ref, w_ref, o_ref, u_ref):
    @pl.when(pl.program_id(1) == 0)
    def _():
        x = x_ref[...]
        ms = jnp.mean(x * x, axis=-1, keepdims=True)
        u_ref[...] = _bf(x * lax.rsqrt(ms + RMS_EPS) * g_ref[...])

    o_ref[...] = jnp.dot(u_ref[...], w_ref[...], preferred_element_type=F32)


def _norm_matmul(x, gain, w, tm, tn):
    t, d = x.shape
    n = w.shape[1]
    return pl.pallas_call(
        _norm_matmul_kernel,
        grid=(t // tm, n // tn),
        in_specs=[pl.BlockSpec((tm, d), lambda i, j: (i, 0)),
                  pl.BlockSpec((1, d), lambda i, j: (0, 0)),
                  pl.BlockSpec((d, tn), lambda i, j: (0, j))],
        out_specs=[pl.BlockSpec((tm, tn), lambda i, j: (i, j)),
                   pl.BlockSpec((tm, d), lambda i, j: (i, 0))],
        out_shape=[jax.ShapeDtypeStruct((t, n), F32), jax.ShapeDtypeStruct((t, d), BF16)],
        compiler_params=_cparams(("parallel", "arbitrary")),
        name="in_proj",
    )(x, gain, w)


MXU_TILE = 256


HALO = 16


def _projection_pieces(u_ref, uprev_ref, unext_ref, w_ref, p_ref):
    u = jnp.concatenate([uprev_ref[...], u_ref[...], unext_ref[...]], axis=0)
    ncol = w_ref.shape[1]

    def piece(c0):
        c1 = min(c0 + MXU_TILE, ncol)
        p_ref[:, c0:c1] = jnp.dot(u, w_ref[:, c0:c1], preferred_element_type=F32)

    return [functools.partial(piece, c0) for c0 in range(0, ncol, MXU_TILE)]


def _halo_tile(p_ref, pos, tiles_per_seq):
    n = p_ref.shape[0]
    prev = jnp.where(pos == 0, 0.0, p_ref[0:HALO, :])
    nxt = jnp.where(pos == tiles_per_seq - 1, 0.0, p_ref[n - HALO:n, :])
    return jnp.concatenate([prev, p_ref[HALO:n - HALO, :], nxt], axis=0)


def _front_kernel(math, tiles_per_seq, n_params, n_out, *refs):
    u_ref, uprev_ref, unext_ref, w_ref = refs[:4]
    params = refs[4:4 + n_params]
    outs = refs[4 + n_params:4 + n_params + n_out]
    pa_ref, pb_ref = refs[4 + n_params + n_out:]
    i = pl.program_id(0)
    pos = (i + tiles_per_seq - 1) % tiles_per_seq

    @pl.when(i == 0)
    def _():
        pb_ref[...] = jnp.zeros_like(pb_ref)

    def step(p_new, p_old):
        pieces = _projection_pieces(u_ref, uprev_ref, unext_ref, w_ref, p_new)

        per_tick = -(-len(pieces) // (math.ticks + 1))

        def tick():
            for _ in range(min(per_tick, len(pieces))):
                pieces.pop(0)()

        tick()
        math(_halo_tile(p_old, pos, tiles_per_seq), params, outs, tick)
        assert not pieces

    @pl.when(i % 2 == 0)
    def _():
        step(pa_ref, pb_ref)

    @pl.when(i % 2 == 1)
    def _():
        step(pb_ref, pa_ref)


def _front(math, name, u, w, params, param_specs, out_specs, out_shapes, seq, ts):
    t, d = u.shape
    n = t // ts
    nhalo = t // HALO
    rpb = ts // HALO
    cur = lambda i: jnp.minimum(i, n - 1)
    shifted = lambda spec: pl.BlockSpec(spec.block_shape, lambda i, f=spec.index_map: f(jnp.maximum(i - 1, 0)))
    return pl.pallas_call(
        functools.partial(_front_kernel, math, seq // ts, len(params), len(out_specs)),
        grid=(n + 1,),
        in_specs=[pl.BlockSpec((ts, d), lambda i: (cur(i), 0)),
                  pl.BlockSpec((HALO, d), lambda i: (jnp.maximum(cur(i) * rpb - 1, 0), 0)),
                  pl.BlockSpec((HALO, d), lambda i: (jnp.minimum((cur(i) + 1) * rpb, nhalo - 1), 0)),
                  pl.BlockSpec(w.shape, lambda i: (0, 0), pipeline_mode=pl.Buffered(1))] + param_specs,
        out_specs=[shifted(s) for s in out_specs],
        out_shape=out_shapes,
        scratch_shapes=[pltpu.VMEM((ts + 2 * HALO, w.shape[1]), F32)] * 2,
        compiler_params=pltpu.CompilerParams(dimension_semantics=("arbitrary",), vmem_limit_bytes=FRONT_VMEM_LIMIT),
        name=name,
    )(u, u, u, w, *params)


def _rw_prep_math(ext, params, outs, tick):
    mu_ref, w0_ref, w2_ref, a0_ref, a2_ref, g2_ref, kk_ref, ka_ref, rk_ref, bd_ref = params
    v_o, g_o, bonus_o, rf_o, kf_o, bf_o, kkf_o, gcf_o, rb_o, kb_o, bb_o, kkb_o, gcb_o = outs
    n = ext.shape[0]
    ts = n - 2 * HALO
    w = RW_WIDTH

    def mixed(c0, c1):
        e = ext[:, c0:c1]
        body = slice(HALO, HALO + ts)
        mu = mu_ref[:, c0:c1]
        out = e[body] * (1.0 - mu) + (pltpu.roll(e, 1, axis=0)[body] + pltpu.roll(e, n - 1, axis=0)[body]) * (0.5 * mu)
        tick()
        return out

    lora = mixed(3 * w, RW_BLOCK)
    wlin = _mm(jnp.tanh(lora[:, 0:LANES]), w2_ref[...]) + w0_ref[...]
    alin = _mm(lora[:, LANES:2 * LANES], a2_ref[...]) + a0_ref[...]
    g = _mm(_sigmoid(lora[:, 2 * LANES:4 * LANES]), g2_ref[...])
    r = mixed(0, w)
    k = mixed(w, 2 * w)
    v = mixed(2 * w, 3 * w)
    lw = -RW_DECAY_SCALE * _sigmoid(wlin)
    tick()
    a = _sigmoid(alin)
    tick()

    bd = bd_ref[...]
    kscaled = k * kk_ref[...]
    kkn = kscaled * lax.rsqrt(_seg_sum(kscaled * kscaled, bd) + L2_EPS)
    tick()
    ka = ka_ref[...]
    a_f = a[:, 0:w]
    a_b = a[:, w:2 * w]
    k_f = k * (1.0 + (a_f - 1.0) * ka)
    k_b = k * (1.0 + (a_b - 1.0) * ka)
    tick()
    bonus = _seg_sum(r * (0.5 * (k_f + k_b)) * rk_ref[...], bd) * v

    v_o[...] = _bf(v)
    g_o[...] = g
    bonus_o[...] = bonus
    tick()

    rt = lax.broadcasted_iota(jnp.int32, (ts, ts), 0)
    rs = lax.broadcasted_iota(jnp.int32, (ts, ts), 1)
    same_chunk = (rt // CHUNK) == (rs // CHUNK)
    for d, (k_d, a_d, r_o, k_o, b_o, kk_o, gc_o) in enumerate(((k_f, a_f, rf_o, kf_o, bf_o, kkf_o, gcf_o),
                                                                (k_b, a_b, rb_o, kb_o, bb_o, kkb_o, gcb_o))):
        lw_d = lw[:, d * w:(d + 1) * w]
        before = (rs <= rt) if d == 0 else (rs >= rt)
        cum = _mm01_left(jnp.where(same_chunk & before, 1.0, 0.0).astype(BF16), lw_d, terms=2)
        tick()
        g_inv = jnp.exp(-cum)
        r_o[...] = _bf(r * jnp.exp(cum))
        tick()
        k_o[...] = _bf(k_d * g_inv)
        b_o[...] = _bf(kkn * a_d * g_inv)
        tick()
        kk_o[...] = _bf(kkn * jnp.exp(cum - lw_d))
        for c in range(ts // CHUNK):
            last = c * CHUNK + (CHUNK - 1 if d == 0 else 0)
            gc_o[c] = jnp.exp(cum[last:last + 1, :])
        tick()


_rw_prep_math.ticks = 17


def _rw_front(u, w_rw, seq, ts, mu, w0, w2, a0, a2, g2, k_k, k_a, r_k, bd64):
    t = u.shape[0]
    params = [mu, w0, w2, a0, a2, g2, k_k, k_a, r_k, bd64]
    cpt = ts // CHUNK
    big = pl.BlockSpec((ts, RW_WIDTH), lambda i: (i, 0))
    tot = pl.BlockSpec((cpt, 1, RW_WIDTH), lambda i: (i, 0, 0))
    wide = lambda dt: jax.ShapeDtypeStruct((t, RW_WIDTH), dt)
    tots = jax.ShapeDtypeStruct((t // CHUNK, 1, RW_WIDTH), F32)
    direction = [wide(BF16)] * 4 + [tots]
    return _front(_rw_prep_math, "rw_front", u, w_rw, params,
                  [pl.BlockSpec(a.shape, lambda i: (0, 0)) for a in params],
                  [big] * 3 + ([big] * 4 + [tot]) * 2,
                  [wide(BF16), wide(F32), wide(F32)] + direction * 2, seq, ts)


RW_GROUP = 4
RW_GROUP_W = RW_GROUP * RW_HEAD_DIM


def _rw_scan_kernel(reverse, v_ref, r_ref, k_ref, b_ref, kk_ref, gc_ref, *rest):
    acc_ref = rest[0] if len(rest) == 3 else None
    y_ref, s_ref = rest[-2:]
    @pl.when(pl.program_id(0) == 0)
    def _():
        s_ref[...] = jnp.zeros_like(s_ref)

    gw = RW_GROUP_W
    nb, rows, width = r_ref.shape
    nblk = width // gw
    nch = rows // CHUNK
    strict, incl, eye = _side_masks(reverse, RW_GROUP)
    ri = lax.broadcasted_iota(jnp.int32, (gw, gw), 0) // CHUNK
    ci = lax.broadcasted_iota(jnp.int32, (gw, gw), 1) // CHUNK
    bd_mask = ri == ci
    order = list(range(nch - 1, -1, -1) if reverse else range(nch))
    lanes = [(bi, blk) for bi in range(nb) for blk in range(nblk)]
    chains = [(bi, blk, c) for c in order for (bi, blk) in lanes]

    def tile(ref, bi, blk, c):
        return ref[bi, c * CHUNK:(c + 1) * CHUNK, blk * gw:(blk + 1) * gw]

    r_t = {ch: tile(r_ref, *ch) for ch in chains}
    k_t = {ch: tile(k_ref, *ch) for ch in chains}
    b_t = {ch: tile(b_ref, *ch) for ch in chains}
    kk_t = {ch: tile(kk_ref, *ch) for ch in chains}
    vv = {ch: tile(v_ref, *ch) for ch in chains}

    a_v, a_kb, a_rb, pp, tt = {}, {}, {}, {}, {}
    for ch in chains:
        lhs = jnp.concatenate([kk_t[ch], r_t[ch]], axis=0)
        a_k = _mm_nt(lhs, _stack_masked(k_t[ch]))
        a_b = _mm_nt(lhs, _stack_masked(b_t[ch]))
        a_v[ch] = jnp.concatenate([jnp.where(strict, a_k[:CHUNK], 0.0), jnp.where(incl, a_k[CHUNK:], 0.0)], axis=0)
        a_rb[ch] = jnp.where(incl, a_b[CHUNK:], 0.0)
        a_kb[ch] = jnp.where(strict, a_b[:CHUNK], 0.0)
    for ch in chains:
        pp[ch], tt[ch] = _inverse_init(a_kb[ch], eye)
    for _ in range(INVERSE_DOUBLINGS):
        for ch in chains:
            pp[ch], tt[ch] = _inverse_double(pp[ch], tt[ch])
    for ch in chains:
        tt[ch] = _inverse_last(pp[ch], tt[ch])

    w_mat, u0, y0 = {}, {}, {}
    for ch in chains:
        av = _mm(a_v[ch], _stack_masked(vv[ch]))
        y0[ch] = av[CHUNK:]
        u0[ch] = av[:CHUNK]
        w_mat[ch] = _mm(tt[ch], _stack_masked(kk_t[ch]))
    for ch in chains:
        u0[ch] = _mm(tt[ch], _stack_masked(u0[ch]))

    state = {(bi, blk): s_ref[bi * nblk + blk] for (bi, blk) in lanes}
    for c in order:
        ws = {ln: _mm_nt(jnp.concatenate([_bf(w_mat[(*ln, c)]), r_t[(*ln, c)]], axis=0), state[ln]) for ln in lanes}
        u = {ln: ws[ln][:CHUNK] + u0[(*ln, c)] for ln in lanes}
        ds = {ln: _mm_tn(jnp.concatenate([vv[(*ln, c)], _bf(-u[ln])], axis=0),
                         jnp.concatenate([k_t[(*ln, c)], b_t[(*ln, c)]], axis=0)) for ln in lanes}
        for (bi, blk) in lanes:
            ln = (bi, blk)
            state[ln] = (state[ln] + jnp.where(bd_mask, ds[ln], 0.0)) * gc_ref[bi, c, :, blk * gw:(blk + 1) * gw]
            y = ws[ln][CHUNK:] + y0[(bi, blk, c)] - _mm(a_rb[(bi, blk, c)], _stack_masked(u[ln]))
            where = (bi, slice(c * CHUNK, (c + 1) * CHUNK), slice(blk * gw, (blk + 1) * gw))
            y_ref[where] = y if acc_ref is None else y + acc_ref[where]
    for (bi, blk) in lanes:
        s_ref[bi * nblk + blk] = state[(bi, blk)]


def _rw_scan(v, r_t, k_t, b_t, kk_t, gc, batch, tt, reverse, acc=None):
    t = v.shape[0]
    seq = t // batch
    nt = seq // tt
    cpt = tt // CHUNK
    rowi = (lambda c: nt - 1 - c) if reverse else (lambda c: c)
    spec = pl.BlockSpec((batch, tt, RW_WIDTH), lambda c: (0, rowi(c), 0))
    seqs = lambda a: a.reshape(batch, seq, RW_WIDTH)
    extra = [] if acc is None else [seqs(acc)]
    y = pl.pallas_call(
        functools.partial(_rw_scan_kernel, reverse),
        grid=(nt,),
        in_specs=([spec] * 5 + [pl.BlockSpec((batch, cpt, 1, RW_WIDTH), lambda c: (0, rowi(c), 0, 0))]
                  + [spec] * len(extra)),
        out_specs=spec,
        out_shape=jax.ShapeDtypeStruct((batch, seq, RW_WIDTH), F32),
        scratch_shapes=[pltpu.VMEM((batch * RW_WIDTH // RW_GROUP_W, RW_GROUP_W, RW_GROUP_W), F32)],
        compiler_params=_cparams(("arbitrary",)),
        name="rw_scan_bwd" if reverse else "rw_scan_fwd",
    )(seqs(v), seqs(r_t), seqs(k_t), seqs(b_t), seqs(kk_t), gc.reshape(batch, seq // CHUNK, 1, RW_WIDTH), *extra)
    return y.reshape(t, RW_WIDTH)


def _rw_post_math(y, bonus, g, gn_w, gn_b, bd):
    inv_n = 1.0 / RW_HEAD_DIM
    mean = _seg_sum(y, bd) * inv_n
    yc = y - mean
    var = _seg_sum(yc * yc, bd) * inv_n
    yn = yc * lax.rsqrt(var + RW_GN_EPS) * gn_w + gn_b
    return _bf((yn + bonus) * g)


def _gdn_prep_math(ext_all, params, outs, tick):
    cw_ref, alog_ref, dtb_ref, ones_ref = params
    q_o, k_o, v_o, gates_o = outs
    n = ext_all.shape[0]
    ts = n - 2 * HALO
    half = GDN_CONV // 2

    def conv_silu(c0, c1):
        e = ext_all[:, c0:c1]
        acc = None
        for j in range(GDN_CONV):
            shift = (half - j) % n
            xs = e if shift == 0 else pltpu.roll(e, shift, axis=0)
            term = cw_ref[j:j + 1, c0:c1] * xs[HALO:HALO + ts, :]
            acc = term if acc is None else acc + term
        tick()
        out = acc * _sigmoid(acc)
        tick()
        return out

    ones = ones_ref[...]
    qw = GDN_QK_WIDTH
    q = conv_silu(0, qw)
    q_o[...] = q * lax.rsqrt(_seg_sum(q * q, ones) + L2_EPS) * (GDN_HEAD_DIM ** -0.5)
    k = conv_silu(qw, 2 * qw)
    k_o[...] = k * lax.rsqrt(_seg_sum(k * k, ones) + L2_EPS)
    v_o[:, 0:qw] = conv_silu(2 * qw, 3 * qw)
    v_o[:, qw:2 * qw] = conv_silu(3 * qw, 4 * qw)

    gx = ext_all[HALO:HALO + ts, GDN_CONV_CH:]
    lane = lax.broadcasted_iota(jnp.int32, gx.shape, 1)
    log_decay = -jnp.exp(alog_ref[...]) * _softplus(gx + dtb_ref[...])
    gates_o[...] = jnp.where(lane < 2 * GDN_V_HEADS, log_decay, _sigmoid(gx))


_gdn_prep_math.ticks = 8


def _gdn_front(u, w_gdn, seq, ts, conv_w, a_log, dt_bias, ones128):
    t = u.shape[0]
    params = [conv_w, a_log, dt_bias, ones128]
    widths = (GDN_QK_WIDTH, GDN_QK_WIDTH, GDN_V_WIDTH, LANES)
    return _front(_gdn_prep_math, "gdn_front", u, w_gdn, params,
                  [pl.BlockSpec(a.shape, lambda i: (0, 0)) for a in params],
                  [pl.BlockSpec((ts, wd), lambda i: (i, 0)) for wd in widths],
                  [jax.ShapeDtypeStruct((t, wd), F32) for wd in widths], seq, ts)


def _gdn_scan_kernel(reverse, g_off, b_off, q_ref, k_ref, v_ref, gates_ref, *rest):
    acc_ref = rest[0] if len(rest) == 3 else None
    o_ref, s_ref = rest[-2:]

    @pl.when(pl.program_id(0) == 0)
    def _():
        s_ref[...] = jnp.zeros_like(s_ref)

    nb = q_ref.shape[0]
    nch = q_ref.shape[1] // CHUNK
    strict, incl, _ = _side_masks(reverse, 2)
    eye4 = _side_masks(reverse, 4)[2]
    tri = _tri_incl(reverse)
    s_i = lax.broadcasted_iota(jnp.int32, (CHUNK, 2 * CHUNK), 0)
    t_i = lax.broadcasted_iota(jnp.int32, (CHUNK, 2 * CHUNK), 1) % CHUNK
    tri_t2 = jnp.where((s_i >= t_i) if reverse else (s_i <= t_i), 1.0, 0.0).astype(BF16)
    first = lax.broadcasted_iota(jnp.int32, (CHUNK, 2 * CHUNK), 1) < CHUNK
    zero = jnp.zeros((CHUNK, LANES), F32)
    zero_s = jnp.zeros((LANES, LANES), F32)
    last = 0 if reverse else CHUNK - 1
    order = list(range(nch - 1, -1, -1) if reverse else range(nch))
    slots = [(bi, c) for c in order for bi in range(nb)]
    chains = [(bi, j, c) for (bi, c) in slots for j in range(GDN_QK_HEADS)]

    def rows(c):
        return slice(c * CHUNK, (c + 1) * CHUNK)

    gates, ccol, crow = {}, {}, {}
    for sl in slots:
        gates[sl] = gates_ref[sl[0], rows(sl[1]), :]
        ccol[sl] = _mm01_left(tri, gates[sl])
        crow[sl] = _mm01_tn(gates[sl], tri_t2)

    q, k, qk_pair, a_pair = {}, {}, {}, {}
    for ch in chains:
        bi, j, c = ch
        sl = (bi, c)
        i0, i1 = g_off + 2 * j, g_off + 2 * j + 1
        q[ch] = q_ref[bi, rows(c), j * LANES:(j + 1) * LANES]
        k[ch] = k_ref[bi, rows(c), j * LANES:(j + 1) * LANES]
        diff = (jnp.where(first, ccol[sl][:, i0:i0 + 1], ccol[sl][:, i1:i1 + 1])
                - jnp.where(first, crow[sl][i0:i0 + 1, :], crow[sl][i1:i1 + 1, :]))
        decay = jnp.where(incl, jnp.exp(jnp.minimum(diff, 0.0)), 0.0)
        beta_pair = jnp.where(first, gates[sl][:, b_off + 2 * j:b_off + 2 * j + 1],
                              gates[sl][:, b_off + 2 * j + 1:b_off + 2 * j + 2])
        kdup = jnp.concatenate([k[ch], k[ch]], axis=0)
        kq = _mm_nt(jnp.concatenate([k[ch], q[ch]], axis=0), kdup)
        qk_pair[ch] = kq[CHUNK:] * decay
        a_pair[ch] = jnp.where(strict, beta_pair * kq[:CHUNK] * decay, 0.0)

    quads = [(bi, jj, c) for (bi, c) in slots for jj in range(GDN_QK_HEADS // 2)]
    pp, tq = {}, {}
    for (bi, jj, c) in quads:
        pp[(bi, jj, c)], tq[(bi, jj, c)] = _inverse_init(
            jnp.concatenate([a_pair[(bi, 2 * jj, c)], a_pair[(bi, 2 * jj + 1, c)]], axis=1), eye4)
    for _ in range(INVERSE_DOUBLINGS):
        for qd in quads:
            pp[qd], tq[qd] = _inverse_double(pp[qd], tq[qd])
    tt = {}
    for (bi, jj, c) in quads:
        t4 = _inverse_last(pp[(bi, jj, c)], tq[(bi, jj, c)])
        tt[(bi, 2 * jj, c)] = t4[:, :LANES]
        tt[(bi, 2 * jj + 1, c)] = t4[:, LANES:]

    sol, wq = {}, {}
    for ch in chains:
        bi, j, c = ch
        parts = []
        for e in range(2):
            h = 2 * j + e
            beta = gates[(bi, c)][:, b_off + h:b_off + h + 1]
            eg = jnp.exp(ccol[(bi, c)][:, g_off + h:g_off + h + 1])
            v_e = v_ref[bi, rows(c), h * LANES:(h + 1) * LANES]
            blocks = [zero] * 4
            blocks[2 * e] = v_e * beta
            blocks[2 * e + 1] = k[ch] * (beta * eg)
            parts.append(jnp.concatenate(blocks, axis=1))
            wq[(bi, j, c, e)] = q[ch] * eg
        sol[ch] = _mm(tt[ch], jnp.concatenate(parts, axis=0))

    pairs = [(bi, j) for bi in range(nb) for j in range(GDN_QK_HEADS)]
    heads = [(bi, j, e) for (bi, j) in pairs for e in range(2)]
    state = {(bi, j, e): s_ref[(bi * GDN_QK_HEADS + j) * 2 + e] for (bi, j, e) in heads}
    for c in order:
        ws = {}
        for (bi, j) in pairs:
            so = sol[(bi, j, c)]
            lhs = jnp.concatenate([jnp.concatenate([so[:, LANES:2 * LANES], so[:, 3 * LANES:]], axis=1),
                                   jnp.concatenate([wq[(bi, j, c, 0)], wq[(bi, j, c, 1)]], axis=1)], axis=0)
            s_bd = jnp.concatenate([jnp.concatenate([state[(bi, j, 0)], zero_s], axis=1),
                                    jnp.concatenate([zero_s, state[(bi, j, 1)]], axis=1)], axis=0)
            ws[(bi, j)] = _mm(lhs, s_bd)
        vn = {(bi, j, e): (sol[(bi, j, c)][:, 2 * e * LANES:(2 * e + 1) * LANES]
                           - ws[(bi, j)][:CHUNK, e * LANES:(e + 1) * LANES]) for (bi, j, e) in heads}
        for (bi, j, e) in heads:
            cc = ccol[(bi, c)][:, g_off + 2 * j + e:g_off + 2 * j + e + 1]
            tot = cc[last:last + 1, :]
            kg = k[(bi, j, c)] * jnp.exp(tot - cc)
            state[(bi, j, e)] = state[(bi, j, e)] * jnp.exp(tot) + _mm_tn(kg, vn[(bi, j, e)])
        for (bi, j) in pairs:
            vn_bd = jnp.concatenate([jnp.concatenate([vn[(bi, j, 0)], zero], axis=1),
                                     jnp.concatenate([zero, vn[(bi, j, 1)]], axis=1)], axis=0)
            o_intra = _mm(qk_pair[(bi, j, c)], vn_bd)
            where = (bi, rows(c), slice(2 * j * LANES, (2 * j + 2) * LANES))
            o = ws[(bi, j)][CHUNK:] + o_intra
            o_ref[where] = o if acc_ref is None else o + acc_ref[where]
    for (bi, j, e) in heads:
        s_ref[(bi * GDN_QK_HEADS + j) * 2 + e] = state[(bi, j, e)]


def _gdn_scan(q, k, v, gates, batch, tt, reverse, acc=None):
    t = q.shape[0]
    seq = t // batch
    nt = seq // tt
    rowi = (lambda c: nt - 1 - c) if reverse else (lambda c: c)
    g_off = GDN_V_HEADS if reverse else 0
    b_off = 2 * GDN_V_HEADS + g_off
    spec = lambda width: pl.BlockSpec((batch, tt, width), lambda c: (0, rowi(c), 0))
    seqs = lambda a: a.reshape(batch, seq, a.shape[-1])
    extra = [] if acc is None else [seqs(acc)]
    o = pl.pallas_call(
        functools.partial(_gdn_scan_kernel, reverse, g_off, b_off),
        grid=(nt,),
        in_specs=[spec(GDN_QK_WIDTH), spec(GDN_QK_WIDTH), spec(GDN_V_WIDTH), spec(LANES)]
        + [spec(GDN_V_WIDTH)] * len(extra),
        out_specs=spec(GDN_V_WIDTH),
        out_shape=jax.ShapeDtypeStruct((batch, seq, GDN_V_WIDTH), F32),
        scratch_shapes=[pltpu.VMEM((batch * GDN_V_HEADS, LANES, LANES), F32)],
        compiler_params=_cparams(("arbitrary",)),
        name="gdn_scan_bwd" if reverse else "gdn_scan_fwd",
    )(seqs(q), seqs(k), seqs(v), seqs(gates), *extra)
    return o.reshape(t, GDN_V_WIDTH)


def _gdn_post_math(o, z, norm_w, ones):
    ms = _seg_sum(o * o, ones) * (1.0 / GDN_HEAD_DIM)
    return _bf(o * lax.rsqrt(ms + GDN_NORM_EPS) * norm_w * (z * _sigmoid(z)))


MERGE_SUB = 128


def _mix_out_kernel(y_ref, bonus_ref, g_ref, gnw_ref, gnb_ref, bd_ref, o_ref, z_ref, nw_ref, ones_ref,
                    gr_ref, gg_ref, pa_ref, pb_ref, x_ref, wo_ref, gpost_ref, gnext_ref, h_ref, u_ref):
    parts = []
    for s in range(y_ref.shape[0] // MERGE_SUB):
        rows = slice(s * MERGE_SUB, (s + 1) * MERGE_SUB)
        y_rw = _rw_post_math(y_ref[rows, :], bonus_ref[rows, :], g_ref[rows, :],
                             gnw_ref[...], gnb_ref[...], bd_ref[...])
        y_gdn = _gdn_post_math(o_ref[rows, :], z_ref[rows, :], nw_ref[...], ones_ref[...])
        a = jnp.dot(y_rw, pa_ref[...], preferred_element_type=F32)
        b = jnp.dot(y_gdn, pb_ref[...], preferred_element_type=F32)
        parts.append(_bf(_sigmoid(gr_ref[rows, :]) * a + _sigmoid(gg_ref[rows, :]) * b))
    m = jnp.concatenate(parts, axis=0)
    yo = jnp.dot(m, wo_ref[...], preferred_element_type=F32)
    ms = jnp.mean(yo * yo, axis=-1, keepdims=True)
    h = x_ref[...] + yo * lax.rsqrt(ms + RMS_EPS) * gpost_ref[...]
    h_ref[...] = h
    ms2 = jnp.mean(h * h, axis=-1, keepdims=True)
    u_ref[...] = _bf(h * lax.rsqrt(ms2 + RMS_EPS) * gnext_ref[...])


def _mix_out(y, bonus, g, gn_w, gn_b, bd64, o, p, norm_w, ones128, w_a, w_b, x, w_out, gain, gain_next, tm):
    t, d = x.shape
    wide = pl.BlockSpec((tm, RW_WIDTH), lambda i: (i, 0))
    row = lambda width: pl.BlockSpec((1, width), lambda i: (0, 0))
    const = lambda a: pl.BlockSpec(a.shape, lambda i: (0, 0))
    weight = lambda a: pl.BlockSpec(a.shape, lambda i: (0, 0), pipeline_mode=pl.Buffered(1))
    pcol = lambda width, off: pl.BlockSpec((tm, width), lambda i: (i, off // width))
    tile = pl.BlockSpec((tm, d), lambda i: (i, 0))
    return pl.pallas_call(
        _mix_out_kernel,
        grid=(t // tm,),
        in_specs=[wide, wide, wide, row(RW_WIDTH), row(RW_WIDTH), const(bd64),
                  wide, pcol(GDN_V_WIDTH, P_GZ), row(GDN_V_WIDTH), const(ones128),
                  pcol(d, P_MG_RW), pcol(d, P_MG_GDN), weight(w_a), weight(w_b),
                  tile, weight(w_out), row(d), row(d)],
        out_specs=[tile, tile],
        out_shape=[jax.ShapeDtypeStruct((t, d), F32), jax.ShapeDtypeStruct((t, d), BF16)],
        compiler_params=pltpu.CompilerParams(dimension_semantics=("parallel",), vmem_limit_bytes=FRONT_VMEM_LIMIT),
        name="mix_out",
    )(y, bonus, g, gn_w, gn_b, bd64, o, p, norm_w, ones128, p, p, w_a, w_b, x, w_out, gain, gain_next)


def _ffn_kernel(h_ref, u_ref, wg_ref, wu_ref, wd_ref, gpost_ref, o_ref):
    j = pl.program_id(1)

    @pl.when(j == 0)
    def _():
        o_ref[...] = jnp.zeros_like(o_ref)

    u = u_ref[...]
    gate = jnp.dot(u, wg_ref[...], preferred_element_type=F32)
    up = jnp.dot(u, wu_ref[...], preferred_element_type=F32)
    f = _bf(gate * _sigmoid(gate) * up)
    o_ref[...] += jnp.dot(f, wd_ref[...], preferred_element_type=F32)

    @pl.when(j == pl.num_programs(1) - 1)
    def _():
        y = o_ref[...]
        ms = jnp.mean(y * y, axis=-1, keepdims=True)
        o_ref[...] = h_ref[...] + y * lax.rsqrt(ms + RMS_EPS) * gpost_ref[...]


def _ffn(h, u, wg, wu, wd, gpost, tm, tf):
    t, d = h.shape
    f = wg.shape[1]
    assert t % tm == 0 and f % tf == 0, (t, tm, f, tf)
    return pl.pallas_call(
        _ffn_kernel,
        grid=(t // tm, f // tf),
        in_specs=[pl.BlockSpec((tm, d), lambda i, j: (i, 0)),
                  pl.BlockSpec((tm, d), lambda i, j: (i, 0)),
                  pl.BlockSpec((d, tf), lambda i, j: (0, j)),
                  pl.BlockSpec((d, tf), lambda i, j: (0, j)),
                  pl.BlockSpec((tf, d), lambda i, j: (j, 0)),
                  pl.BlockSpec((1, d), lambda i, j: (0, 0))],
        out_specs=pl.BlockSpec((tm, d), lambda i, j: (i, 0)),
        out_shape=jax.ShapeDtypeStruct((t, d), F32),
        compiler_params=_cparams(("parallel", "arbitrary")),
        name="ffn",
    )(h, u, wg, wu, wd, gpost)


def _pack_w_in(w_in):
    d = w_in.shape[0]
    rw = w_in[:, :RW_IN]
    gdn = w_in[:, RW_IN:RW_IN + GDN_IN]
    gates = w_in[:, RW_IN + GDN_IN:]
    z = lambda n: jnp.zeros((d, n), w_in.dtype)
    cat = lambda cols: _bf(jnp.concatenate(cols, axis=1))
    w_rw = cat([rw, z(RW_BLOCK - RW_IN)])
    w_gdn = cat([gdn[:, :GDN_CONV_CH], gdn[:, GDN_CONV_CH + GDN_V_WIDTH:], z(LANES - 4 * GDN_V_HEADS)])
    w_rest = cat([gates, gdn[:, GDN_CONV_CH:GDN_CONV_CH + GDN_V_WIDTH]])
    return w_rw, w_gdn, w_rest


def _block_diag2(a, b):
    z = jnp.zeros_like(a)
    return jnp.concatenate([jnp.concatenate([a, z], axis=1), jnp.concatenate([z, b], axis=1)], axis=0)


def _group_ones(group):
    i = jnp.arange(MXU_TILE)
    return (i[:, None] // group == i[None, :] // group).astype(BF16)


def _forward(x, norm_pre_mix, w_in, rw_shift_mu, rw_w0_f, rw_w2_f, rw_w0_b, rw_w2_b, rw_a0_f, rw_a2_f, rw_a0_b,
             rw_a2_b, rw_g2, rw_k_k, rw_k_a, rw_r_k, rw_gn_w, rw_gn_b, gdn_conv_w, gdn_a_log_f, gdn_dt_bias_f,
             gdn_a_log_b, gdn_dt_bias_b, gdn_norm_w, w_branch_rw, w_branch_gdn, w_out, norm_post_mix, norm_pre_ffn,
             w_ffn_gate, w_ffn_up, w_ffn_down, norm_post_ffn, *, tiles):
    batch, seq, d = x.shape
    t = batch * seq
    xf = x.reshape(t, d)
    row = lambda a: a.reshape(1, -1).astype(F32)
    bd64 = _group_ones(RW_HEAD_DIM)
    ones128 = _group_ones(GDN_HEAD_DIM)

    w_rw, w_gdn, w_rest = _pack_w_in(w_in)
    p, u_in = _norm_matmul(xf, row(norm_pre_mix), w_rest, tiles["in_tm"], tiles["in_tn"])

    mu = jnp.pad(row(rw_shift_mu), ((0, 0), (0, RW_BLOCK - RW_IN)))
    w0 = jnp.concatenate([row(rw_w0_f), row(rw_w0_b)], axis=1)
    a0 = jnp.concatenate([row(rw_a0_f), row(rw_a0_b)], axis=1)
    w2 = _bf(_block_diag2(rw_w2_f, rw_w2_b))
    a2 = _bf(_block_diag2(rw_a2_f, rw_a2_b))
    g2 = _bf(jnp.pad(rw_g2, ((0, 2 * LANES - RW_GATE_LORA), (0, 0))))
    v, g, bonus, *scaled = _rw_front(
        u_in, w_rw, seq, tiles["prep_ts"], mu, w0, w2, a0, a2, g2, row(rw_k_k), row(rw_k_a), row(rw_r_k), bd64)
    y_f = _rw_scan(v, *scaled[:5], batch, tiles["rw_tt"], False)
    y = _rw_scan(v, *scaled[5:], batch, tiles["rw_tt"], True, acc=y_f)

    conv_w = jnp.pad(gdn_conv_w.astype(F32), ((0, SUBLANES - GDN_CONV), (0, 0)))
    nv = GDN_V_HEADS
    a_log = jnp.pad(jnp.concatenate([gdn_a_log_f, gdn_a_log_b]).reshape(1, -1), ((0, 0), (0, LANES - 2 * nv)))
    dt_bias = jnp.pad(jnp.concatenate([gdn_dt_bias_f, gdn_dt_bias_b]).reshape(1, -1), ((0, 0), (0, LANES - 2 * nv)))
    q, k, vg, gates = _gdn_front(u_in, w_gdn, seq, tiles["prep_ts"], conv_w, a_log.astype(F32),
                                 dt_bias.astype(F32), ones128)
    o_f = _gdn_scan(q, k, vg, gates, batch, tiles["gdn_tt"], False)
    o = _gdn_scan(q, k, vg, gates, batch, tiles["gdn_tt"], True, acc=o_f)
    nw = jnp.tile(row(gdn_norm_w), (1, GDN_V_HEADS))

    h1, u_ffn = _mix_out(y, bonus, g, row(rw_gn_w), row(rw_gn_b), bd64, o, p, nw, ones128,
                         _bf(w_branch_rw), _bf(w_branch_gdn), xf, _bf(w_out), row(norm_post_mix),
                         row(norm_pre_ffn), tiles["mix_tm"])
    h2 = _ffn(h1, u_ffn, _bf(w_ffn_gate), _bf(w_ffn_up), _bf(w_ffn_down), row(norm_post_ffn),
              tiles["ffn_tm"], tiles["ffn_tf"])
    return h2.reshape(batch, seq, d)


def _tiles(seq):
    pick = lambda want: min(want, seq)
    return dict(in_tm=pick(1024), in_tn=1024, prep_ts=pick(256), rw_tt=pick(256), gdn_tt=pick(256),
                mix_tm=pick(256), ffn_tm=pick(512), ffn_tf=512)


def kernel(x, norm_pre_mix, w_in, rw_shift_mu, rw_w0_f, rw_w2_f, rw_w0_b, rw_w2_b, rw_a0_f, rw_a2_f, rw_a0_b, rw_a2_b, rw_g2, rw_k_k, rw_k_a, rw_r_k, rw_gn_w, rw_gn_b, gdn_conv_w, gdn_a_log_f, gdn_dt_bias_f, gdn_a_log_b, gdn_dt_bias_b, gdn_norm_w, w_branch_rw, w_branch_gdn, w_out, norm_post_mix, norm_pre_ffn, w_ffn_gate, w_ffn_up, w_ffn_down, norm_post_ffn):
    args = [a[0] for a in (norm_pre_mix, w_in, rw_shift_mu, rw_w0_f, rw_w2_f, rw_w0_b, rw_w2_b, rw_a0_f, rw_a2_f,
                           rw_a0_b, rw_a2_b, rw_g2, rw_k_k, rw_k_a, rw_r_k, rw_gn_w, rw_gn_b, gdn_conv_w,
                           gdn_a_log_f, gdn_dt_bias_f, gdn_a_log_b, gdn_dt_bias_b, gdn_norm_w, w_branch_rw,
                           w_branch_gdn, w_out, norm_post_mix, norm_pre_ffn, w_ffn_gate, w_ffn_up, w_ffn_down,
                           norm_post_ffn)]
    assert norm_pre_mix.shape[0] == 1, "one layer"
    return _forward(x, *args, tiles=_tiles(x.shape[1]))
```

```python
import functools

import jax
import jax.numpy as jnp
from jax import lax
from jax.experimental import pallas as pl
from jax.experimental.pallas import tpu as pltpu

F32 = jnp.float32
BF16 = jnp.bfloat16

D_MODEL = 2048
RMS_EPS = 1e-6

RW_HEADS = 16
RW_HEAD_DIM = 64
RW_WIDTH = 1024
RW_LORA = 64
RW_GATE_LORA = 160
RW_GN_EPS = 64e-5
RW_IN = 3488
RW_DECAY_SCALE = 0.6065306597126334

GDN_QK_HEADS = 4
GDN_V_HEADS = 8
GDN_HEAD_DIM = 128
GDN_QK_WIDTH = 512
GDN_V_WIDTH = 1024
GDN_CONV_CH = 2048
GDN_CONV = 5
GDN_NORM_EPS = 1e-6
GDN_IN = 3104
L2_EPS = 1e-6

FFN_HIDDEN = 5632

CHUNK = 64
LANES = 128
SUBLANES = 8

RW_BLOCK = 3584
GDN_BLOCK = 2176
P_MG_RW = 0
P_MG_GDN = 2048
P_GZ = 4096
P_WIDTH = 5120

VMEM_LIMIT = 48 * 1024 * 1024
FRONT_VMEM_LIMIT = 56 * 1024 * 1024


def _cparams(sem):
    return pltpu.CompilerParams(dimension_semantics=sem, vmem_limit_bytes=VMEM_LIMIT)


def _bf(x):
    return x.astype(BF16)


def _mm(a, b):
    return jnp.dot(_bf(a), _bf(b), preferred_element_type=F32)


def _mm_nt(a, b):
    return lax.dot_general(_bf(a), _bf(b), (((1,), (1,)), ((), ())), preferred_element_type=F32)


def _mm_tn(a, b):
    return lax.dot_general(_bf(a), _bf(b), (((0,), (0,)), ((), ())), preferred_element_type=F32)


def _split3(x):
    x1 = _bf(x)
    r1 = x - x1.astype(F32)
    x2 = _bf(r1)
    x3 = _bf(r1 - x2.astype(F32))
    return x1, x2, x3


def _mm01_left(m01, x, terms=3):
    d = functools.partial(jnp.dot, preferred_element_type=F32)
    if terms == 2:
        x1 = _bf(x)
        return d(m01, x1) + d(m01, _bf(x - x1.astype(F32)))
    x1, x2, x3 = _split3(x)
    return d(m01, x1) + d(m01, x2) + d(m01, x3)


def _mm01_right(x, m01):
    return jnp.dot(_bf(x), m01, preferred_element_type=F32)


def _mm01_tn(x, m01):
    x1, x2, x3 = _split3(x)
    d = functools.partial(lax.dot_general, dimension_numbers=(((0,), (0,)), ((), ())), preferred_element_type=F32)
    return d(x1, m01) + d(x2, m01) + d(x3, m01)


def _seg_sum(x, ones_bd):
    blk = ones_bd.shape[0]
    n = x.shape[1] // blk
    parts = [_mm01_right(x[:, j * blk:(j + 1) * blk], ones_bd) for j in range(n)]
    return parts[0] if n == 1 else jnp.concatenate(parts, axis=1)


def _sigmoid(x):
    return 1.0 / (1.0 + jnp.exp(-x))


def _softplus(x):
    return jnp.maximum(x, 0.0) + jnp.log(1.0 + jnp.exp(-jnp.abs(x)))


def _stack_masked(x):
    group = lax.broadcasted_iota(jnp.int32, x.shape, 1) // CHUNK
    return jnp.concatenate([jnp.where(group == h, x, 0.0) for h in range(x.shape[1] // CHUNK)], axis=0)


def _side_masks(reverse, n):
    t = lax.broadcasted_iota(jnp.int32, (CHUNK, n * CHUNK), 0)
    s = lax.broadcasted_iota(jnp.int32, (CHUNK, n * CHUNK), 1) % CHUNK
    if reverse:
        return s > t, s >= t, s == t
    return s < t, s <= t, s == t


def _tri_incl(reverse):
    t = lax.broadcasted_iota(jnp.int32, (CHUNK, CHUNK), 0)
    s = lax.broadcasted_iota(jnp.int32, (CHUNK, CHUNK), 1)
    m = (s >= t) if reverse else (s <= t)
    return jnp.where(m, 1.0, 0.0).astype(BF16)


INVERSE_DOUBLINGS = CHUNK.bit_length() - 3


def _inverse_init(a_side, eye):
    n = -a_side
    return _mm(n, _stack_masked(n)), jnp.where(eye, 1.0, 0.0) + n


def _inverse_double(p, t):
    pt = _mm(jnp.concatenate([p, t], axis=0), _stack_masked(p))
    return pt[:CHUNK], t + pt[CHUNK:]


def _inverse_last(p, t):
    return t + _mm(t, _stack_masked(p))


def _norm_matmul_kernel(x_ref, g_ref, w_ref, o_ref, u_ref):
    @pl.when(pl.program_id(1) == 0)
    def _():
        x = x_ref[...]
        ms = jnp.mean(x * x, axis=-1, keepdims=True)
        u_ref[...] = _bf(x * lax.rsqrt(ms + RMS_EPS) * g_ref[...])

    o_ref[...] = jnp.dot(u_ref[...], w_ref[...], preferred_element_type=F32)


def _norm_matmul(x, gain, w, tm, tn):
    t, d = x.shape
    n = w.shape[1]
    return pl.pallas_call(
        _norm_matmul_kernel,
        grid=(t // tm, n // tn),
        in_specs=[pl.BlockSpec((tm, d), lambda i, j: (i, 0)),
                  pl.BlockSpec((1, d), lambda i, j: (0, 0)),
                  pl.BlockSpec((d, tn), lambda i, j: (0, j))],
        out_specs=[pl.BlockSpec((tm, tn), lambda i, j: (i, j)),
                   pl.BlockSpec((tm, d), lambda i, j: (i, 0))],
        out_shape=[jax.ShapeDtypeStruct((t, n), F32), jax.ShapeDtypeStruct((t, d), BF16)],
        compiler_params=_cparams(("parallel", "arbitrary")),
        name="in_proj",
    )(x, gain, w)


MXU_TILE = 256


HALO = 16


def _projection_pieces(u_ref, uprev_ref, unext_ref, w_ref, p_ref):
    u = jnp.concatenate([uprev_ref[...], u_ref[...], unext_ref[...]], axis=0)
    ncol = w_ref.shape[1]

    def piece(c0):
        c1 = min(c0 + MXU_TILE, ncol)
        p_ref[:, c0:c1] = jnp.dot(u, w_ref[:, c0:c1], preferred_element_type=F32)

    return [functools.partial(piece, c0) for c0 in range(0, ncol, MXU_TILE)]


def _halo_tile(p_ref, pos, tiles_per_seq):
    n = p_ref.shape[0]
    prev = jnp.where(pos == 0, 0.0, p_ref[0:HALO, :])
    nxt = jnp.where(pos == tiles_per_seq - 1, 0.0, p_ref[n - HALO:n, :])
    return jnp.concatenate([prev, p_ref[HALO:n - HALO, :], nxt], axis=0)


def _front_kernel(math, tiles_per_seq, n_params, n_out, *refs):
    u_ref, uprev_ref, unext_ref, w_ref = refs[:4]
    params = refs[4:4 + n_params]
    outs = refs[4 + n_params:4 + n_params + n_out]
    pa_ref, pb_ref = refs[4 + n_params + n_out:]
    i = pl.program_id(0)
    pos = (i + tiles_per_seq - 1) % tiles_per_seq

    @pl.when(i == 0)
    def _():
        pb_ref[...] = jnp.zeros_like(pb_ref)

    def step(p_new, p_old):
        pieces = _projection_pieces(u_ref, uprev_ref, unext_ref, w_ref, p_new)

        per_tick = -(-len(pieces) // (math.ticks + 1))

        def tick():
            for _ in range(min(per_tick, len(pieces))):
                pieces.pop(0)()

        tick()
        math(_halo_tile(p_old, pos, tiles_per_seq), params, outs, tick)
        assert not pieces

    @pl.when(i % 2 == 0)
    def _():
        step(pa_ref, pb_ref)

    @pl.when(i % 2 == 1)
    def _():
        step(pb_ref, pa_ref)


def _front(math, name, u, w, params, param_specs, out_specs, out_shapes, seq, ts):
    t, d = u.shape
    n = t // ts
    nhalo = t // HALO
    rpb = ts // HALO
    cur = lambda i: jnp.minimum(i, n - 1)
    shifted = lambda spec: pl.BlockSpec(spec.block_shape, lambda i, f=spec.index_map: f(jnp.maximum(i - 1, 0)))
    return pl.pallas_call(
        functools.partial(_front_kernel, math, seq // ts, len(params), len(out_specs)),
        grid=(n + 1,),
        in_specs=[pl.BlockSpec((ts, d), lambda i: (cur(i), 0)),
                  pl.BlockSpec((HALO, d), lambda i: (jnp.maximum(cur(i) * rpb - 1, 0), 0)),
                  pl.BlockSpec((HALO, d), lambda i: (jnp.minimum((cur(i) + 1) * rpb, nhalo - 1), 0)),
                  pl.BlockSpec(w.shape, lambda i: (0, 0), pipeline_mode=pl.Buffered(1))] + param_specs,
        out_specs=[shifted(s) for s in out_specs],
        out_shape=out_shapes,
        scratch_shapes=[pltpu.VMEM((ts + 2 * HALO, w.shape[1]), F32)] * 2,
        compiler_params=pltpu.CompilerParams(dimension_semantics=("arbitrary",), vmem_limit_bytes=FRONT_VMEM_LIMIT),
        name=name,
    )(u, u, u, w, *params)


def _rw_prep_math(ext, params, outs, tick):
    mu_ref, w0_ref, w2_ref, a0_ref, a2_ref, g2_ref, kk_ref, ka_ref, rk_ref, bd_ref = params
    v_o, g_o, bonus_o, rf_o, kf_o, bf_o, kkf_o, gcf_o, rb_o, kb_o, bb_o, kkb_o, gcb_o = outs
    n = ext.shape[0]
    ts = n - 2 * HALO
    w = RW_WIDTH

    def mixed(c0, c1):
        e = ext[:, c0:c1]
        body = slice(HALO, HALO + ts)
        mu = mu_ref[:, c0:c1]
        out = e[body] * (1.0 - mu) + (pltpu.roll(e, 1, axis=0)[body] + pltpu.roll(e, n - 1, axis=0)[body]) * (0.5 * mu)
        tick()
        return out

    lora = mixed(3 * w, RW_BLOCK)
    wlin = _mm(jnp.tanh(lora[:, 0:LANES]), w2_ref[...]) + w0_ref[...]
    alin = _mm(lora[:, LANES:2 * LANES], a2_ref[...]) + a0_ref[...]
    g = _mm(_sigmoid(lora[:, 2 * LANES:4 * LANES]), g2_ref[...])
    r = mixed(0, w)
    k = mixed(w, 2 * w)
    v = mixed(2 * w, 3 * w)
    lw = -RW_DECAY_SCALE * _sigmoid(wlin)
    tick()
    a = _sigmoid(alin)
    tick()

    bd = bd_ref[...]
    kscaled = k * kk_ref[...]
    kkn = kscaled * lax.rsqrt(_seg_sum(kscaled * kscaled, bd) + L2_EPS)
    tick()
    ka = ka_ref[...]
    a_f = a[:, 0:w]
    a_b = a[:, w:2 * w]
    k_f = k * (1.0 + (a_f - 1.0) * ka)
    k_b = k * (1.0 + (a_b - 1.0) * ka)
    tick()
    bonus = _seg_sum(r * (0.5 * (k_f + k_b)) * rk_ref[...], bd) * v

    v_o[...] = _bf(v)
    g_o[...] = g
    bonus_o[...] = bonus
    tick()

    rt = lax.broadcasted_iota(jnp.int32, (ts, ts), 0)
    rs = lax.broadcasted_iota(jnp.int32, (ts, ts), 1)
    same_chunk = (rt // CHUNK) == (rs // CHUNK)
    for d, (k_d, a_d, r_o, k_o, b_o, kk_o, gc_o) in enumerate(((k_f, a_f, rf_o, kf_o, bf_o, kkf_o, gcf_o),
                                                                (k_b, a_b, rb_o, kb_o, bb_o, kkb_o, gcb_o))):
        lw_d = lw[:, d * w:(d + 1) * w]
        before = (rs <= rt) if d == 0 else (rs >= rt)
        cum = _mm01_left(jnp.where(same_chunk & before, 1.0, 0.0).astype(BF16), lw_d, terms=2)
        tick()
        g_inv = jnp.exp(-cum)
        r_o[...] = _bf(r * jnp.exp(cum))
        tick()
        k_o[...] = _bf(k_d * g_inv)
        b_o[...] = _bf(kkn * a_d * g_inv)
        tick()
        kk_o[...] = _bf(kkn * jnp.exp(cum - lw_d))
        for c in range(ts // CHUNK):
            last = c * CHUNK + (CHUNK - 1 if d == 0 else 0)
            gc_o[c] = jnp.exp(cum[last:last + 1, :])
        tick()


_rw_prep_math.ticks = 17


def _rw_front(u, w_rw, seq, ts, mu, w0, w2, a0, a2, g2, k_k, k_a, r_k, bd64):
    t = u.shape[0]
    params = [mu, w0, w2, a0, a2, g2, k_k, k_a, r_k, bd64]
    cpt = ts // CHUNK
    big = pl.BlockSpec((ts, RW_WIDTH), lambda i: (i, 0))
    tot = pl.BlockSpec((cpt, 1, RW_WIDTH), lambda i: (i, 0, 0))
    wide = lambda dt: jax.ShapeDtypeStruct((t, RW_WIDTH), dt)
    tots = jax.ShapeDtypeStruct((t // CHUNK, 1, RW_WIDTH), F32)
    direction = [wide(BF16)] * 4 + [tots]
    return _front(_rw_prep_math, "rw_front", u, w_rw, params,
                  [pl.BlockSpec(a.shape, lambda i: (0, 0)) for a in params],
                  [big] * 3 + ([big] * 4 + [tot]) * 2,
                  [wide(BF16), wide(F32), wide(F32)] + direction * 2, seq, ts)


RW_GROUP = 4
RW_GROUP_W = RW_GROUP * RW_HEAD_DIM


def _rw_scan_kernel(reverse, v_ref, r_ref, k_ref, b_ref, kk_ref, gc_ref, *rest):
    acc_ref = rest[0] if len(rest) == 3 else None
    y_ref, s_ref = rest[-2:]
    @pl.when(pl.program_id(0) == 0)
    def _():
        s_ref[...] = jnp.zeros_like(s_ref)

    gw = RW_GROUP_W
    nb, rows, width = r_ref.shape
    nblk = width // gw
    nch = rows // CHUNK
    strict, incl, eye = _side_masks(reverse, RW_GROUP)
    ri = lax.broadcasted_iota(jnp.int32, (gw, gw), 0) // CHUNK
    ci = lax.broadcasted_iota(jnp.int32, (gw, gw), 1) // CHUNK
    bd_mask = ri == ci
    order = list(range(nch - 1, -1, -1) if reverse else range(nch))
    lanes = [(bi, blk) for bi in range(nb) for blk in range(nblk)]
    chains = [(bi, blk, c) for c in order for (bi, blk) in lanes]

    def tile(ref, bi, blk, c):
        return ref[bi, c * CHUNK:(c + 1) * CHUNK, blk * gw:(blk + 1) * gw]

    r_t = {ch: tile(r_ref, *ch) for ch in chains}
    k_t = {ch: tile(k_ref, *ch) for ch in chains}
    b_t = {ch: tile(b_ref, *ch) for ch in chains}
    kk_t = {ch: tile(kk_ref, *ch) for ch in chains}
    vv = {ch: tile(v_ref, *ch) for ch in chains}

    a_v, a_kb, a_rb, pp, tt = {}, {}, {}, {}, {}
    for ch in chains:
        lhs = jnp.concatenate([kk_t[ch], r_t[ch]], axis=0)
        a_k = _mm_nt(lhs, _stack_masked(k_t[ch]))
        a_b = _mm_nt(lhs, _stack_masked(b_t[ch]))
        a_v[ch] = jnp.concatenate([jnp.where(strict, a_k[:CHUNK], 0.0), jnp.where(incl, a_k[CHUNK:], 0.0)], axis=0)
        a_rb[ch] = jnp.where(incl, a_b[CHUNK:], 0.0)
        a_kb[ch] = jnp.where(strict, a_b[:CHUNK], 0.0)
    for ch in chains:
        pp[ch], tt[ch] = _inverse_init(a_kb[ch], eye)
    for _ in range(INVERSE_DOUBLINGS):
        for ch in chains:
            pp[ch], tt[ch] = _inverse_double(pp[ch], tt[ch])
    for ch in chains:
        tt[ch] = _inverse_last(pp[ch], tt[ch])

    w_mat, u0, y0 = {}, {}, {}
    for ch in chains:
        av = _mm(a_v[ch], _stack_masked(vv[ch]))
        y0[ch] = av[CHUNK:]
        u0[ch] = av[:CHUNK]
        w_mat[ch] = _mm(tt[ch], _stack_masked(kk_t[ch]))
    for ch in chains:
        u0[ch] = _mm(tt[ch], _stack_masked(u0[ch]))

    state = {(bi, blk): s_ref[bi * nblk + blk] for (bi, blk) in lanes}
    for c in order:
        ws = {ln: _mm_nt(jnp.concatenate([_bf(w_mat[(*ln, c)]), r_t[(*ln, c)]], axis=0), state[ln]) for ln in lanes}
        u = {ln: ws[ln][:CHUNK] + u0[(*ln, c)] for ln in lanes}
        ds = {ln: _mm_tn(jnp.concatenate([vv[(*ln, c)], _bf(-u[ln])], axis=0),
                         jnp.concatenate([k_t[(*ln, c)], b_t[(*ln, c)]], axis=0)) for ln in lanes}
        for (bi, blk) in lanes:
            ln = (bi, blk)
            state[ln] = (state[ln] + jnp.where(bd_mask, ds[ln], 0.0)) * gc_ref[bi, c, :, blk * gw:(blk + 1) * gw]
            y = ws[ln][CHUNK:] + y0[(bi, blk, c)] - _mm(a_rb[(bi, blk, c)], _stack_masked(u[ln]))
            where = (bi, slice(c * CHUNK, (c + 1) * CHUNK), slice(blk * gw, (blk + 1) * gw))
            y_ref[where] = y if acc_ref is None else y + acc_ref[where]
    for (bi, blk) in lanes:
        s_ref[bi * nblk + blk] = state[(bi, blk)]


def _rw_scan(v, r_t, k_t, b_t, kk_t, gc, batch, tt, reverse, acc=None):
    t = v.shape[0]
    seq = t // batch
    nt = seq // tt
    cpt = tt // CHUNK
    rowi = (lambda c: nt - 1 - c) if reverse else (lambda c: c)
    spec = pl.BlockSpec((batch, tt, RW_WIDTH), lambda c: (0, rowi(c), 0))
    seqs = lambda a: a.reshape(batch, seq, RW_WIDTH)
    extra = [] if acc is None else [seqs(acc)]
    y = pl.pallas_call(
        functools.partial(_rw_scan_kernel, reverse),
        grid=(nt,),
        in_specs=([spec] * 5 + [pl.BlockSpec((batch, cpt, 1, RW_WIDTH), lambda c: (0, rowi(c), 0, 0))]
                  + [spec] * len(extra)),
        out_specs=spec,
        out_shape=jax.ShapeDtypeStruct((batch, seq, RW_WIDTH), F32),
        scratch_shapes=[pltpu.VMEM((batch * RW_WIDTH // RW_GROUP_W, RW_GROUP_W, RW_GROUP_W), F32)],
        compiler_params=_cparams(("arbitrary",)),
        name="rw_scan_bwd" if reverse else "rw_scan_fwd",
    )(seqs(v), seqs(r_t), seqs(k_t), seqs(b_t), seqs(kk_t), gc.reshape(batch, seq // CHUNK, 1, RW_WIDTH), *extra)
    return y.reshape(t, RW_WIDTH)


def _rw_post_math(y, bonus, g, gn_w, gn_b, bd):
    inv_n = 1.0 / RW_HEAD_DIM
    mean = _seg_sum(y, bd) * inv_n
    yc = y - mean
    var = _seg_sum(yc * yc, bd) * inv_n
    yn = yc * lax.rsqrt(var + RW_GN_EPS) * gn_w + gn_b
    return _bf((yn + bonus) * g)


def _gdn_front_kernel(tiles_per_seq, u_ref, uprev_ref, unext_ref, w_ref, cw_ref, alog_ref, dtb_ref, ones_ref,
                      q_o, k_o, v_o, gates_o):
    pos = pl.program_id(0) % tiles_per_seq
    ts = u_ref.shape[0]
    n = ts + 2 * HALO
    half = GDN_CONV // 2
    u = jnp.concatenate([uprev_ref[...], u_ref[...], unext_ref[...]], axis=0)

    def project(c0, c1):
        e = jnp.dot(u, w_ref[:, c0:c1], preferred_element_type=F32)
        prev = jnp.where(pos == 0, 0.0, e[0:HALO])
        nxt = jnp.where(pos == tiles_per_seq - 1, 0.0, e[n - HALO:n])
        return jnp.concatenate([prev, e[HALO:n - HALO], nxt], axis=0)

    def conv_silu(e, c0, c1):
        acc = None
        for j in range(GDN_CONV):
            shift = (half - j) % n
            xs = e if shift == 0 else pltpu.roll(e, shift, axis=0)
            term = cw_ref[j:j + 1, c0:c1] * xs[HALO:HALO + ts, :]
            acc = term if acc is None else acc + term
        return acc * _sigmoid(acc)

    ones = ones_ref[...]
    qw = GDN_QK_WIDTH
    e_q = project(0, qw)
    e_k = project(qw, 2 * qw)
    q = conv_silu(e_q, 0, qw)
    e_v0 = project(2 * qw, 3 * qw)
    q_o[...] = q * lax.rsqrt(_seg_sum(q * q, ones) + L2_EPS) * (GDN_HEAD_DIM ** -0.5)
    k = conv_silu(e_k, qw, 2 * qw)
    e_v1 = project(3 * qw, 4 * qw)
    k_o[...] = k * lax.rsqrt(_seg_sum(k * k, ones) + L2_EPS)
    v_o[:, 0:qw] = conv_silu(e_v0, 2 * qw, 3 * qw)
    gx = jnp.dot(u_ref[...], w_ref[:, GDN_CONV_CH:], preferred_element_type=F32)
    v_o[:, qw:2 * qw] = conv_silu(e_v1, 3 * qw, 4 * qw)

    lane = lax.broadcasted_iota(jnp.int32, gx.shape, 1)
    log_decay = -jnp.exp(alog_ref[...]) * _softplus(gx + dtb_ref[...])
    gates_o[...] = jnp.where(lane < 2 * GDN_V_HEADS, log_decay, _sigmoid(gx))


def _gdn_front(u, w_gdn, seq, ts, conv_w, a_log, dt_bias, ones128):
    t, d = u.shape
    nhalo = t // HALO
    rpb = ts // HALO
    params = [conv_w, a_log, dt_bias, ones128]
    widths = (GDN_QK_WIDTH, GDN_QK_WIDTH, GDN_V_WIDTH, LANES)
    return pl.pallas_call(
        functools.partial(_gdn_front_kernel, seq // ts),
        grid=(t // ts,),
        in_specs=[pl.BlockSpec((ts, d), lambda i: (i, 0)),
                  pl.BlockSpec((HALO, d), lambda i: (jnp.maximum(i * rpb - 1, 0), 0)),
                  pl.BlockSpec((HALO, d), lambda i: (jnp.minimum((i + 1) * rpb, nhalo - 1), 0)),
                  pl.BlockSpec(w_gdn.shape, lambda i: (0, 0), pipeline_mode=pl.Buffered(1))]
        + [pl.BlockSpec(a.shape, lambda i: (0, 0)) for a in params],
        out_specs=[pl.BlockSpec((ts, wd), lambda i: (i, 0)) for wd in widths],
        out_shape=[jax.ShapeDtypeStruct((t, wd), F32) for wd in widths],
        compiler_params=_cparams(("parallel",)),
        name="gdn_front",
    )(u, u, u, w_gdn, *params)


def _gdn_scan_kernel(reverse, g_off, b_off, q_ref, k_ref, v_ref, gates_ref, *rest):
    acc_ref = rest[0] if len(rest) == 3 else None
    o_ref, s_ref = rest[-2:]

    @pl.when(pl.program_id(0) == 0)
    def _():
        s_ref[...] = jnp.zeros_like(s_ref)

    nb = q_ref.shape[0]
    nch = q_ref.shape[1] // CHUNK
    strict, incl, _ = _side_masks(reverse, 2)
    eye4 = _side_masks(reverse, 4)[2]
    tri = _tri_incl(reverse)
    s_i = lax.broadcasted_iota(jnp.int32, (CHUNK, 2 * CHUNK), 0)
    t_i = lax.broadcasted_iota(jnp.int32, (CHUNK, 2 * CHUNK), 1) % CHUNK
    tri_t2 = jnp.where((s_i >= t_i) if reverse else (s_i <= t_i), 1.0, 0.0).astype(BF16)
    first = lax.broadcasted_iota(jnp.int32, (CHUNK, 2 * CHUNK), 1) < CHUNK
    zero = jnp.zeros((CHUNK, LANES), F32)
    zero_s = jnp.zeros((LANES, LANES), F32)
    last = 0 if reverse else CHUNK - 1
    order = list(range(nch - 1, -1, -1) if reverse else range(nch))
    slots = [(bi, c) for c in order for bi in range(nb)]
    chains = [(bi, j, c) for (bi, c) in slots for j in range(GDN_QK_HEADS)]

    def rows(c):
        return slice(c * CHUNK, (c + 1) * CHUNK)

    gates, ccol, crow = {}, {}, {}
    for sl in slots:
        gates[sl] = gates_ref[sl[0], rows(sl[1]), :]
        ccol[sl] = _mm01_left(tri, gates[sl])
        crow[sl] = _mm01_tn(gates[sl], tri_t2)

    q, k, qk_pair, a_pair = {}, {}, {}, {}
    for ch in chains:
        bi, j, c = ch
        sl = (bi, c)
        i0, i1 = g_off + 2 * j, g_off + 2 * j + 1
        q[ch] = q_ref[bi, rows(c), j * LANES:(j + 1) * LANES]
        k[ch] = k_ref[bi, rows(c), j * LANES:(j + 1) * LANES]
        diff = (jnp.where(first, ccol[sl][:, i0:i0 + 1], ccol[sl][:, i1:i1 + 1])
                - jnp.where(first, crow[sl][i0:i0 + 1, :], crow[sl][i1:i1 + 1, :]))
        decay = jnp.where(incl, jnp.exp(jnp.minimum(diff, 0.0)), 0.0)
        beta_pair = jnp.where(first, gates[sl][:, b_off + 2 * j:b_off + 2 * j + 1],
                              gates[sl][:, b_off + 2 * j + 1:b_off + 2 * j + 2])
        kdup = jnp.concatenate([k[ch], k[ch]], axis=0)
        kq = _mm_nt(jnp.concatenate([k[ch], q[ch]], axis=0), kdup)
        qk_pair[ch] = kq[CHUNK:] * decay
        a_pair[ch] = jnp.where(strict, beta_pair * kq[:CHUNK] * decay, 0.0)

    quads = [(bi, jj, c) for (bi, c) in slots for jj in range(GDN_QK_HEADS // 2)]
    pp, tq = {}, {}
    for (bi, jj, c) in quads:
        pp[(bi, jj, c)], tq[(bi, jj, c)] = _inverse_init(
            jnp.concatenate([a_pair[(bi, 2 * jj, c)], a_pair[(bi, 2 * jj + 1, c)]], axis=1), eye4)
    for _ in range(INVERSE_DOUBLINGS):
        for qd in quads:
            pp[qd], tq[qd] = _inverse_double(pp[qd], tq[qd])
    tt = {}
    for (bi, jj, c) in quads:
        t4 = _inverse_last(pp[(bi, jj, c)], tq[(bi, jj, c)])
        tt[(bi, 2 * jj, c)] = t4[:, :LANES]
        tt[(bi, 2 * jj + 1, c)] = t4[:, LANES:]

    sol, wq = {}, {}
    for ch in chains:
        bi, j, c = ch
        parts = []
        for e in range(2):
            h = 2 * j + e
            beta = gates[(bi, c)][:, b_off + h:b_off + h + 1]
            eg = jnp.exp(ccol[(bi, c)][:, g_off + h:g_off + h + 1])
            v_e = v_ref[bi, rows(c), h * LANES:(h + 1) * LANES]
            blocks = [zero] * 4
            blocks[2 * e] = v_e * beta
            blocks[2 * e + 1] = k[ch] * (beta * eg)
            parts.append(jnp.concatenate(blocks, axis=1))
            wq[(bi, j, c, e)] = q[ch] * eg
        sol[ch] = _mm(tt[ch], jnp.concatenate(parts, axis=0))

    pairs = [(bi, j) for bi in range(nb) for j in range(GDN_QK_HEADS)]
    heads = [(bi, j, e) for (bi, j) in pairs for e in range(2)]
    state = {(bi, j, e): s_ref[(bi * GDN_QK_HEADS + j) * 2 + e] for (bi, j, e) in heads}
    for c in order:
        ws = {}
        for (bi, j) in pairs:
            so = sol[(bi, j, c)]
            lhs = jnp.concatenate([jnp.concatenate([so[:, LANES:2 * LANES], so[:, 3 * LANES:]], axis=1),
                                   jnp.concatenate([wq[(bi, j, c, 0)], wq[(bi, j, c, 1)]], axis=1)], axis=0)
            s_bd = jnp.concatenate([jnp.concatenate([state[(bi, j, 0)], zero_s], axis=1),
                                    jnp.concatenate([zero_s, state[(bi, j, 1)]], axis=1)], axis=0)
            ws[(bi, j)] = _mm(lhs, s_bd)
        vn = {(bi, j, e): (sol[(bi, j, c)][:, 2 * e * LANES:(2 * e + 1) * LANES]
                           - ws[(bi, j)][:CHUNK, e * LANES:(e + 1) * LANES]) for (bi, j, e) in heads}
        for (bi, j, e) in heads:
            cc = ccol[(bi, c)][:, g_off + 2 * j + e:g_off + 2 * j + e + 1]
            tot = cc[last:last + 1, :]
            kg = k[(bi, j, c)] * jnp.exp(tot - cc)
            state[(bi, j, e)] = state[(bi, j, e)] * jnp.exp(tot) + _mm_tn(kg, vn[(bi, j, e)])
        for (bi, j) in pairs:
            vn_bd = jnp.concatenate([jnp.concatenate([vn[(bi, j, 0)], zero], axis=1),
                                     jnp.concatenate([zero, vn[(bi, j, 1)]], axis=1)], axis=0)
            o_intra = _mm(qk_pair[(bi, j, c)], vn_bd)
            where = (bi, rows(c), slice(2 * j * LANES, (2 * j + 2) * LANES))
            o = ws[(bi, j)][CHUNK:] + o_intra
            o_ref[where] = o if acc_ref is None else o + acc_ref[where]
    for (bi, j, e) in heads:
        s_ref[(bi * GDN_QK_HEADS + j) * 2 + e] = state[(bi, j, e)]


def _gdn_scan(q, k, v, gates, batch, tt, reverse, acc=None):
    t = q.shape[0]
    seq = t // batch
    nt = seq // tt
    rowi = (lambda c: nt - 1 - c) if reverse else (lambda c: c)
    g_off = GDN_V_HEADS if reverse else 0
    b_off = 2 * GDN_V_HEADS + g_off
    spec = lambda width: pl.BlockSpec((batch, tt, width), lambda c: (0, rowi(c), 0))
    seqs = lambda a: a.reshape(batch, seq, a.shape[-1])
    extra = [] if acc is None else [seqs(acc)]
    o = pl.pallas_call(
        functools.partial(_gdn_scan_kernel, reverse, g_off, b_off),
        grid=(nt,),
        in_specs=[spec(GDN_QK_WIDTH), spec(GDN_QK_WIDTH), spec(GDN_V_WIDTH), spec(LANES)]
        + [spec(GDN_V_WIDTH)] * len(extra),
        out_specs=spec(GDN_V_WIDTH),
        out_shape=jax.ShapeDtypeStruct((batch, seq, GDN_V_WIDTH), F32),
        scratch_shapes=[pltpu.VMEM((batch * GDN_V_HEADS, LANES, LANES), F32)],
        compiler_params=_cparams(("arbitrary",)),
        name="gdn_scan_bwd" if reverse else "gdn_scan_fwd",
    )(seqs(q), seqs(k), seqs(v), seqs(gates), *extra)
    return o.reshape(t, GDN_V_WIDTH)


def _gdn_post_math(o, z, norm_w, ones):
    ms = _seg_sum(o * o, ones) * (1.0 / GDN_HEAD_DIM)
    return _bf(o * lax.rsqrt(ms + GDN_NORM_EPS) * norm_w * (z * _sigmoid(z)))


MERGE_SUB = 128


def _mix_out_kernel(y_ref, bonus_ref, g_ref, gnw_ref, gnb_ref, bd_ref, o_ref, z_ref, nw_ref, ones_ref,
                    gr_ref, gg_ref, pa_ref, pb_ref, x_ref, wo_ref, gpost_ref, gnext_ref, h_ref, u_ref):
    parts = []
    for s in range(y_ref.shape[0] // MERGE_SUB):
        rows = slice(s * MERGE_SUB, (s + 1) * MERGE_SUB)
        y_rw = _rw_post_math(y_ref[rows, :], bonus_ref[rows, :], g_ref[rows, :],
                             gnw_ref[...], gnb_ref[...], bd_ref[...])
        y_gdn = _gdn_post_math(o_ref[rows, :], z_ref[rows, :], nw_ref[...], ones_ref[...])
        a = jnp.dot(y_rw, pa_ref[...], preferred_element_type=F32)
        b = jnp.dot(y_gdn, pb_ref[...], preferred_element_type=F32)
        parts.append(_bf(_sigmoid(gr_ref[rows, :]) * a + _sigmoid(gg_ref[rows, :]) * b))
    m = jnp.concatenate(parts, axis=0)
    yo = jnp.dot(m, wo_ref[...], preferred_element_type=F32)
    ms = jnp.mean(yo * yo, axis=-1, keepdims=True)
    h = x_ref[...] + yo * lax.rsqrt(ms + RMS_EPS) * gpost_ref[...]
    h_ref[...] = h
    ms2 = jnp.mean(h * h, axis=-1, keepdims=True)
    u_ref[...] = _bf(h * lax.rsqrt(ms2 + RMS_EPS) * gnext_ref[...])


def _mix_out(y, bonus, g, gn_w, gn_b, bd64, o, p, norm_w, ones128, w_a, w_b, x, w_out, gain, gain_next, tm):
    t, d = x.shape
    wide = pl.BlockSpec((tm, RW_WIDTH), lambda i: (i, 0))
    row = lambda width: pl.BlockSpec((1, width), lambda i: (0, 0))
    const = lambda a: pl.BlockSpec(a.shape, lambda i: (0, 0))
    weight = lambda a: pl.BlockSpec(a.shape, lambda i: (0, 0), pipeline_mode=pl.Buffered(1))
    pcol = lambda width, off: pl.BlockSpec((tm, width), lambda i: (i, off // width))
    tile = pl.BlockSpec((tm, d), lambda i: (i, 0))
    return pl.pallas_call(
        _mix_out_kernel,
        grid=(t // tm,),
        in_specs=[wide, wide, wide, row(RW_WIDTH), row(RW_WIDTH), const(bd64),
                  wide, pcol(GDN_V_WIDTH, P_GZ), row(GDN_V_WIDTH), const(ones128),
                  pcol(d, P_MG_RW), pcol(d, P_MG_GDN), weight(w_a), weight(w_b),
                  tile, weight(w_out), row(d), row(d)],
        out_specs=[tile, tile],
        out_shape=[jax.ShapeDtypeStruct((t, d), F32), jax.ShapeDtypeStruct((t, d), BF16)],
        compiler_params=pltpu.CompilerParams(dimension_semantics=("parallel",), vmem_limit_bytes=FRONT_VMEM_LIMIT),
        name="mix_out",
    )(y, bonus, g, gn_w, gn_b, bd64, o, p, norm_w, ones128, p, p, w_a, w_b, x, w_out, gain, gain_next)


def _ffn_kernel(h_ref, u_ref, wg_ref, wu_ref, wd_ref, gpost_ref, o_ref):
    j = pl.program_id(1)

    @pl.when(j == 0)
    def _():
        o_ref[...] = jnp.zeros_like(o_ref)

    u = u_ref[...]
    gate = jnp.dot(u, wg_ref[...], preferred_element_type=F32)
    up = jnp.dot(u, wu_ref[...], preferred_element_type=F32)
    f = _bf(gate * _sigmoid(gate) * up)
    o_ref[...] += jnp.dot(f, wd_ref[...], preferred_element_type=F32)

    @pl.when(j == pl.num_programs(1) - 1)
    def _():
        y = o_ref[...]
        ms = jnp.mean(y * y, axis=-1, keepdims=True)
        o_ref[...] = h_ref[...] + y * lax.rsqrt(ms + RMS_EPS) * gpost_ref[...]


def _ffn(h, u, wg, wu, wd, gpost, tm, tf):
    t, d = h.shape
    f = wg.shape[1]
    assert t % tm == 0 and f % tf == 0, (t, tm, f, tf)
    return pl.pallas_call(
        _ffn_kernel,
        grid=(t // tm, f // tf),
        in_specs=[pl.BlockSpec((tm, d), lambda i, j: (i, 0)),
                  pl.BlockSpec((tm, d), lambda i, j: (i, 0)),
                  pl.BlockSpec((d, tf), lambda i, j: (0, j)),
                  pl.BlockSpec((d, tf), lambda i, j: (0, j)),
                  pl.BlockSpec((tf, d), lambda i, j: (j, 0)),
                  pl.BlockSpec((1, d), lambda i, j: (0, 0))],
        out_specs=pl.BlockSpec((tm, d), lambda i, j: (i, 0)),
        out_shape=jax.ShapeDtypeStruct((t, d), F32),
        compiler_params=_cparams(("parallel", "arbitrary")),
        name="ffn",
    )(h, u, wg, wu, wd, gpost)


def _pack_w_in(w_in):
    d = w_in.shape[0]
    rw = w_in[:, :RW_IN]
    gdn = w_in[:, RW_IN:RW_IN + GDN_IN]
    gates = w_in[:, RW_IN + GDN_IN:]
    z = lambda n: jnp.zeros((d, n), w_in.dtype)
    cat = lambda cols: _bf(jnp.concatenate(cols, axis=1))
    w_rw = cat([rw, z(RW_BLOCK - RW_IN)])
    w_gdn = cat([gdn[:, :GDN_CONV_CH], gdn[:, GDN_CONV_CH + GDN_V_WIDTH:], z(LANES - 4 * GDN_V_HEADS)])
    w_rest = cat([gates, gdn[:, GDN_CONV_CH:GDN_CONV_CH + GDN_V_WIDTH]])
    return w_rw, w_gdn, w_rest


def _block_diag2(a, b):
    z = jnp.zeros_like(a)
    return jnp.concatenate([jnp.concatenate([a, z], axis=1), jnp.concatenate([z, b], axis=1)], axis=0)


def _group_ones(group):
    i = jnp.arange(MXU_TILE)
    return (i[:, None] // group == i[None, :] // group).astype(BF16)


def _forward(x, norm_pre_mix, w_in, rw_shift_mu, rw_w0_f, rw_w2_f, rw_w0_b, rw_w2_b, rw_a0_f, rw_a2_f, rw_a0_b,
             rw_a2_b, rw_g2, rw_k_k, rw_k_a, rw_r_k, rw_gn_w, rw_gn_b, gdn_conv_w, gdn_a_log_f, gdn_dt_bias_f,
             gdn_a_log_b, gdn_dt_bias_b, gdn_norm_w, w_branch_rw, w_branch_gdn, w_out, norm_post_mix, norm_pre_ffn,
             w_ffn_gate, w_ffn_up, w_ffn_down, norm_post_ffn, *, tiles):
    batch, seq, d = x.shape
    t = batch * seq
    xf = x.reshape(t, d)
    row = lambda a: a.reshape(1, -1).astype(F32)
    bd64 = _group_ones(RW_HEAD_DIM)
    ones128 = _group_ones(GDN_HEAD_DIM)

    w_rw, w_gdn, w_rest = _pack_w_in(w_in)
    p, u_in = _norm_matmul(xf, row(norm_pre_mix), w_rest, tiles["in_tm"], tiles["in_tn"])

    mu = jnp.pad(row(rw_shift_mu), ((0, 0), (0, RW_BLOCK - RW_IN)))
    w0 = jnp.concatenate([row(rw_w0_f), row(rw_w0_b)], axis=1)
    a0 = jnp.concatenate([row(rw_a0_f), row(rw_a0_b)], axis=1)
    w2 = _bf(_block_diag2(rw_w2_f, rw_w2_b))
    a2 = _bf(_block_diag2(rw_a2_f, rw_a2_b))
    g2 = _bf(jnp.pad(rw_g2, ((0, 2 * LANES - RW_GATE_LORA), (0, 0))))
    v, g, bonus, *scaled = _rw_front(
        u_in, w_rw, seq, tiles["prep_ts"], mu, w0, w2, a0, a2, g2, row(rw_k_k), row(rw_k_a), row(rw_r_k), bd64)
    y_f = _rw_scan(v, *scaled[:5], batch, tiles["rw_tt"], False)
    y = _rw_scan(v, *scaled[5:], batch, tiles["rw_tt"], True, acc=y_f)

    conv_w = jnp.pad(gdn_conv_w.astype(F32), ((0, SUBLANES - GDN_CONV), (0, 0)))
    nv = GDN_V_HEADS
    a_log = jnp.pad(jnp.concatenate([gdn_a_log_f, gdn_a_log_b]).reshape(1, -1), ((0, 0), (0, LANES - 2 * nv)))
    dt_bias = jnp.pad(jnp.concatenate([gdn_dt_bias_f, gdn_dt_bias_b]).reshape(1, -1), ((0, 0), (0, LANES - 2 * nv)))
    q, k, vg, gates = _gdn_front(u_in, w_gdn, seq, tiles["prep_ts"], conv_w, a_log.astype(F32),
                                 dt_bias.astype(F32), ones128)
    o_f = _gdn_scan(q, k, vg, gates, batch, tiles["gdn_tt"], False)
    o = _gdn_scan(q, k, vg, gates, batch, tiles["gdn_tt"], True, acc=o_f)
    nw = jnp.tile(row(gdn_norm_w), (1, GDN_V_HEADS))

    h1, u_ffn = _mix_out(y, bonus, g, row(rw_gn_w), row(rw_gn_b), bd64, o, p, nw, ones128,
                         _bf(w_branch_rw), _bf(w_branch_gdn), xf, _bf(w_out), row(norm_post_mix),
                         row(norm_pre_ffn), tiles["mix_tm"])
    h2 = _ffn(h1, u_ffn, _bf(w_ffn_gate), _bf(w_ffn_up), _bf(w_ffn_down), row(norm_post_ffn),
              tiles["ffn_tm"], tiles["ffn_tf"])
    return h2.reshape(batch, seq, d)


def _tiles(seq):
    pick = lambda want: min(want, seq)
    return dict(in_tm=pick(1024), in_tn=1024, prep_ts=pick(256), rw_tt=pick(256), gdn_tt=pick(256),
                mix_tm=pick(256), ffn_tm=pick(512), ffn_tf=512)


def kernel(x, norm_pre_mix, w_in, rw_shift_mu, rw_w0_f, rw_w2_f, rw_w0_b, rw_w2_b, rw_a0_f, rw_a2_f, rw_a0_b, rw_a2_b, rw_g2, rw_k_k, rw_k_a, rw_r_k, rw_gn_w, rw_gn_b, gdn_conv_w, gdn_a_log_f, gdn_dt_bias_f, gdn_a_log_b, gdn_dt_bias_b, gdn_norm_w, w_branch_rw, w_branch_gdn, w_out, norm_post_mix, norm_pre_ffn, w_ffn_gate, w_ffn_up, w_ffn_down, norm_post_ffn):
    args = [a[0] for a in (norm_pre_mix, w_in, rw_shift_mu, rw_w0_f, rw_w2_f, rw_w0_b, rw_w2_b, rw_a0_f, rw_a2_f,
                           rw_a0_b, rw_a2_b, rw_g2, rw_k_k, rw_k_a, rw_r_k, rw_gn_w, rw_gn_b, gdn_conv_w,
                           gdn_a_log_f, gdn_dt_bias_f, gdn_a_log_b, gdn_dt_bias_b, gdn_norm_w, w_branch_rw,
                           w_branch_gdn, w_out, norm_post_mix, norm_pre_ffn, w_ffn_gate, w_ffn_up, w_ffn_down,
                           norm_post_ffn)]
    assert norm_pre_mix.shape[0] == 1, "one layer"
    return _forward(x, *args, tiles=_tiles(x.shape[1]))
```

```python
import functools

import jax
import jax.numpy as jnp
from jax import lax
from jax.experimental import pallas as pl
from jax.experimental.pallas import tpu as pltpu

F32 = jnp.float32
BF16 = jnp.bfloat16

D_MODEL = 2048
RMS_EPS = 1e-6

RW_HEADS = 16
RW_HEAD_DIM = 64
RW_WIDTH = 1024
RW_LORA = 64
RW_GATE_LORA = 160
RW_GN_EPS = 64e-5
RW_IN = 3488
RW_DECAY_SCALE = 0.6065306597126334

GDN_QK_HEADS = 4
GDN_V_HEADS = 8
GDN_HEAD_DIM = 128
GDN_QK_WIDTH = 512
GDN_V_WIDTH = 1024
GDN_CONV_CH = 2048
GDN_CONV = 5
GDN_NORM_EPS = 1e-6
GDN_IN = 3104
L2_EPS = 1e-6

FFN_HIDDEN = 5632

CHUNK = 64
LANES = 128
SUBLANES = 8

RW_BLOCK = 3584
GDN_BLOCK = 2176
P_MG_RW = 0
P_MG_GDN = 2048
P_GZ = 4096
P_WIDTH = 5120

VMEM_LIMIT = 48 * 1024 * 1024
FRONT_VMEM_LIMIT = 56 * 1024 * 1024


def _cparams(sem):
    return pltpu.CompilerParams(dimension_semantics=sem, vmem_limit_bytes=VMEM_LIMIT)


def _bf(x):
    return x.astype(BF16)


def _mm(a, b):
    return jnp.dot(_bf(a), _bf(b), preferred_element_type=F32)


def _mm_nt(a, b):
    return lax.dot_general(_bf(a), _bf(b), (((1,), (1,)), ((), ())), preferred_element_type=F32)


def _mm_tn(a, b):
    return lax.dot_general(_bf(a), _bf(b), (((0,), (0,)), ((), ())), preferred_element_type=F32)


def _split3(x):
    x1 = _bf(x)
    r1 = x - x1.astype(F32)
    x2 = _bf(r1)
    x3 = _bf(r1 - x2.astype(F32))
    return x1, x2, x3


def _mm01_left(m01, x, terms=3):
    d = functools.partial(jnp.dot, preferred_element_type=F32)
    if terms == 2:
        x1 = _bf(x)
        return d(m01, x1) + d(m01, _bf(x - x1.astype(F32)))
    x1, x2, x3 = _split3(x)
    return d(m01, x1) + d(m01, x2) + d(m01, x3)


def _mm01_right(x, m01):
    return jnp.dot(_bf(x), m01, preferred_element_type=F32)


def _mm01_tn(x, m01):
    x1, x2, x3 = _split3(x)
    d = functools.partial(lax.dot_general, dimension_numbers=(((0,), (0,)), ((), ())), preferred_element_type=F32)
    return d(x1, m01) + d(x2, m01) + d(x3, m01)


def _seg_sum(x, ones_bd):
    blk = ones_bd.shape[0]
    n = x.shape[1] // blk
    parts = [_mm01_right(x[:, j * blk:(j + 1) * blk], ones_bd) for j in range(n)]
    return parts[0] if n == 1 else jnp.concatenate(parts, axis=1)


def _sigmoid(x):
    return 1.0 / (1.0 + jnp.exp(-x))


def _softplus(x):
    return jnp.maximum(x, 0.0) + jnp.log(1.0 + jnp.exp(-jnp.abs(x)))


def _stack_masked(x):
    group = lax.broadcasted_iota(jnp.int32, x.shape, 1) // CHUNK
    return jnp.concatenate([jnp.where(group == h, x, 0.0) for h in range(x.shape[1] // CHUNK)], axis=0)


def _side_masks(reverse, n):
    t = lax.broadcasted_iota(jnp.int32, (CHUNK, n * CHUNK), 0)
    s = lax.broadcasted_iota(jnp.int32, (CHUNK, n * CHUNK), 1) % CHUNK
    if reverse:
        return s > t, s >= t, s == t
    return s < t, s <= t, s == t


def _tri_incl(reverse):
    t = lax.broadcasted_iota(jnp.int32, (CHUNK, CHUNK), 0)
    s = lax.broadcasted_iota(jnp.int32, (CHUNK, CHUNK), 1)
    m = (s >= t) if reverse else (s <= t)
    return jnp.where(m, 1.0, 0.0).astype(BF16)


INVERSE_DOUBLINGS = CHUNK.bit_length() - 3


def _inverse_init(a_side, eye):
    n = -a_side
    return _mm(n, _stack_masked(n)), jnp.where(eye, 1.0, 0.0) + n


def _inverse_double(p, t):
    pt = _mm(jnp.concatenate([p, t], axis=0), _stack_masked(p))
    return pt[:CHUNK], t + pt[CHUNK:]


def _inverse_last(p, t):
    return t + _mm(t, _stack_masked(p))


def _norm_matmul_kernel(x_ref, g_ref, w_ref, o_ref, u_ref):
    @pl.when(pl.program_id(1) == 0)
    def _():
        x = x_ref[...]
        ms = jnp.mean(x * x, axis=-1, keepdims=True)
        u_ref[...] = _bf(x * lax.rsqrt(ms + RMS_EPS) * g_ref[...])

    o_ref[...] = jnp.dot(u_ref[...], w_ref[...], preferred_element_type=F32)


def _norm_matmul(x, gain, w, tm, tn):
    t, d = x.shape
    n = w.shape[1]
    return pl.pallas_call(
        _norm_matmul_kernel,
        grid=(t // tm, n // tn),
        in_specs=[pl.BlockSpec((tm, d), lambda i, j: (i, 0)),
                  pl.BlockSpec((1, d), lambda i, j: (0, 0)),
                  pl.BlockSpec((d, tn), lambda i, j: (0, j))],
        out_specs=[pl.BlockSpec((tm, tn), lambda i, j: (i, j)),
                   pl.BlockSpec((tm, d), lambda i, j: (i, 0))],
        out_shape=[jax.ShapeDtypeStruct((t, n), F32), jax.ShapeDtypeStruct((t, d), BF16)],
        compiler_params=_cparams(("parallel", "arbitrary")),
        name="in_proj",
    )(x, gain, w)


MXU_TILE = 256


HALO = 16
PAD = 8


def _rows_with_halo(u_ref, uprev_ref, unext_ref):
    return jnp.concatenate([uprev_ref[HALO - PAD:HALO, :], unext_ref[0:PAD, :], u_ref[...]], axis=0)


def _projection_pieces(u_ref, uprev_ref, unext_ref, w_ref, p_ref):
    u = _rows_with_halo(u_ref, uprev_ref, unext_ref)
    ncol = w_ref.shape[1]

    def piece(c0):
        c1 = min(c0 + MXU_TILE, ncol)
        p_ref[:, c0:c1] = jnp.dot(u, w_ref[:, c0:c1], preferred_element_type=F32)

    return [functools.partial(piece, c0) for c0 in range(0, ncol, MXU_TILE)]


def _halo_tile(p, pos, tiles_per_seq):
    prev = jnp.where(pos == 0, 0.0, p[0:PAD, :])
    nxt = jnp.where(pos == tiles_per_seq - 1, 0.0, p[PAD:2 * PAD, :])
    return jnp.concatenate([prev, p[2 * PAD:, :], nxt], axis=0)


def _front_kernel(math, tiles_per_seq, n_params, n_out, *refs):
    u_ref, uprev_ref, unext_ref, w_ref = refs[:4]
    params = refs[4:4 + n_params]
    outs = refs[4 + n_params:4 + n_params + n_out]
    pa_ref, pb_ref = refs[4 + n_params + n_out:]
    i = pl.program_id(0)
    pos = (i + tiles_per_seq - 1) % tiles_per_seq

    @pl.when(i == 0)
    def _():
        pb_ref[...] = jnp.zeros_like(pb_ref)

    def step(p_new, p_old):
        pieces = _projection_pieces(u_ref, uprev_ref, unext_ref, w_ref, p_new)

        per_tick = -(-len(pieces) // (math.ticks + 1))

        def tick():
            for _ in range(min(per_tick, len(pieces))):
                pieces.pop(0)()

        tick()
        math(_halo_tile(p_old, pos, tiles_per_seq), params, outs, tick)
        assert not pieces

    @pl.when(i % 2 == 0)
    def _():
        step(pa_ref, pb_ref)

    @pl.when(i % 2 == 1)
    def _():
        step(pb_ref, pa_ref)


def _front(math, name, u, w, params, param_specs, out_specs, out_shapes, seq, ts):
    t, d = u.shape
    n = t // ts
    nhalo = t // HALO
    rpb = ts // HALO
    cur = lambda i: jnp.minimum(i, n - 1)
    shifted = lambda spec: pl.BlockSpec(spec.block_shape, lambda i, f=spec.index_map: f(jnp.maximum(i - 1, 0)))
    return pl.pallas_call(
        functools.partial(_front_kernel, math, seq // ts, len(params), len(out_specs)),
        grid=(n + 1,),
        in_specs=[pl.BlockSpec((ts, d), lambda i: (cur(i), 0)),
                  pl.BlockSpec((HALO, d), lambda i: (jnp.maximum(cur(i) * rpb - 1, 0), 0)),
                  pl.BlockSpec((HALO, d), lambda i: (jnp.minimum((cur(i) + 1) * rpb, nhalo - 1), 0)),
                  pl.BlockSpec(w.shape, lambda i: (0, 0), pipeline_mode=pl.Buffered(1))] + param_specs,
        out_specs=[shifted(s) for s in out_specs],
        out_shape=out_shapes,
        scratch_shapes=[pltpu.VMEM((ts + 2 * PAD, w.shape[1]), F32)] * 2,
        compiler_params=pltpu.CompilerParams(dimension_semantics=("arbitrary",), vmem_limit_bytes=FRONT_VMEM_LIMIT),
        name=name,
    )(u, u, u, w, *params)


def _rw_prep_math(ext, params, outs, tick):
    mu_ref, w0_ref, w2_ref, a0_ref, a2_ref, g2_ref, kk_ref, ka_ref, rk_ref, bd_ref = params
    v_o, g_o, bonus_o, rf_o, kf_o, bf_o, kkf_o, gcf_o, rb_o, kb_o, bb_o, kkb_o, gcb_o = outs
    n = ext.shape[0]
    ts = n - 2 * PAD
    w = RW_WIDTH

    def mixed(c0, c1):
        e = ext[:, c0:c1]
        body = slice(PAD, PAD + ts)
        mu = mu_ref[:, c0:c1]
        out = e[body] * (1.0 - mu) + (pltpu.roll(e, 1, axis=0)[body] + pltpu.roll(e, n - 1, axis=0)[body]) * (0.5 * mu)
        tick()
        return out

    lora = mixed(3 * w, RW_BLOCK)
    wlin = _mm(jnp.tanh(lora[:, 0:LANES]), w2_ref[...]) + w0_ref[...]
    alin = _mm(lora[:, LANES:2 * LANES], a2_ref[...]) + a0_ref[...]
    g = _mm(_sigmoid(lora[:, 2 * LANES:4 * LANES]), g2_ref[...])
    r = mixed(0, w)
    k = mixed(w, 2 * w)
    v = mixed(2 * w, 3 * w)
    lw = -RW_DECAY_SCALE * _sigmoid(wlin)
    tick()
    a = _sigmoid(alin)
    tick()

    bd = bd_ref[...]
    kscaled = k * kk_ref[...]
    kkn = kscaled * lax.rsqrt(_seg_sum(kscaled * kscaled, bd) + L2_EPS)
    tick()
    ka = ka_ref[...]
    a_f = a[:, 0:w]
    a_b = a[:, w:2 * w]
    k_f = k * (1.0 + (a_f - 1.0) * ka)
    k_b = k * (1.0 + (a_b - 1.0) * ka)
    tick()
    bonus = _seg_sum(r * (0.5 * (k_f + k_b)) * rk_ref[...], bd) * v

    v_o[...] = _bf(v)
    g_o[...] = g
    bonus_o[...] = bonus
    tick()

    rt = lax.broadcasted_iota(jnp.int32, (ts, ts), 0)
    rs = lax.broadcasted_iota(jnp.int32, (ts, ts), 1)
    same_chunk = (rt // CHUNK) == (rs // CHUNK)
    for d, (k_d, a_d, r_o, k_o, b_o, kk_o, gc_o) in enumerate(((k_f, a_f, rf_o, kf_o, bf_o, kkf_o, gcf_o),
                                                                (k_b, a_b, rb_o, kb_o, bb_o, kkb_o, gcb_o))):
        lw_d = lw[:, d * w:(d + 1) * w]
        before = (rs <= rt) if d == 0 else (rs >= rt)
        cum = _mm01_left(jnp.where(same_chunk & before, 1.0, 0.0).astype(BF16), lw_d, terms=2)
        tick()
        g_inv = jnp.exp(-cum)
        r_o[...] = _bf(r * jnp.exp(cum))
        tick()
        k_o[...] = _bf(k_d * g_inv)
        b_o[...] = _bf(kkn * a_d * g_inv)
        tick()
        kk_o[...] = _bf(kkn * jnp.exp(cum - lw_d))
        for c in range(ts // CHUNK):
            last = c * CHUNK + (CHUNK - 1 if d == 0 else 0)
            gc_o[c] = jnp.exp(cum[last:last + 1, :])
        tick()


_rw_prep_math.ticks = 17


def _rw_front(u, w_rw, seq, ts, mu, w0, w2, a0, a2, g2, k_k, k_a, r_k, bd64):
    t = u.shape[0]
    params = [mu, w0, w2, a0, a2, g2, k_k, k_a, r_k, bd64]
    cpt = ts // CHUNK
    big = pl.BlockSpec((ts, RW_WIDTH), lambda i: (i, 0))
    tot = pl.BlockSpec((cpt, 1, RW_WIDTH), lambda i: (i, 0, 0))
    wide = lambda dt: jax.ShapeDtypeStruct((t, RW_WIDTH), dt)
    tots = jax.ShapeDtypeStruct((t // CHUNK, 1, RW_WIDTH), F32)
    direction = [wide(BF16)] * 4 + [tots]
    return _front(_rw_prep_math, "rw_front", u, w_rw, params,
                  [pl.BlockSpec(a.shape, lambda i: (0, 0)) for a in params],
                  [big] * 3 + ([big] * 4 + [tot]) * 2,
                  [wide(BF16), wide(F32), wide(F32)] + direction * 2, seq, ts)


RW_GROUP = 4
RW_GROUP_W = RW_GROUP * RW_HEAD_DIM


def _rw_scan_kernel(reverse, v_ref, r_ref, k_ref, b_ref, kk_ref, gc_ref, *rest):
    acc_ref = rest[0] if len(rest) == 3 else None
    y_ref, s_ref = rest[-2:]
    @pl.when(pl.program_id(0) == 0)
    def _():
        s_ref[...] = jnp.zeros_like(s_ref)

    gw = RW_GROUP_W
    nb, rows, width = r_ref.shape
    nblk = width // gw
    nch = rows // CHUNK
    strict, incl, eye = _side_masks(reverse, RW_GROUP)
    ri = lax.broadcasted_iota(jnp.int32, (gw, gw), 0) // CHUNK
    ci = lax.broadcasted_iota(jnp.int32, (gw, gw), 1) // CHUNK
    bd_mask = ri == ci
    order = list(range(nch - 1, -1, -1) if reverse else range(nch))
    lanes = [(bi, blk) for bi in range(nb) for blk in range(nblk)]
    chains = [(bi, blk, c) for c in order for (bi, blk) in lanes]

    def tile(ref, bi, blk, c):
        return ref[bi, c * CHUNK:(c + 1) * CHUNK, blk * gw:(blk + 1) * gw]

    r_t = {ch: tile(r_ref, *ch) for ch in chains}
    k_t = {ch: tile(k_ref, *ch) for ch in chains}
    b_t = {ch: tile(b_ref, *ch) for ch in chains}
    kk_t = {ch: tile(kk_ref, *ch) for ch in chains}
    vv = {ch: tile(v_ref, *ch) for ch in chains}

    a_v, a_kb, a_rb, pp, tt = {}, {}, {}, {}, {}
    for ch in chains:
        lhs = jnp.concatenate([kk_t[ch], r_t[ch]], axis=0)
        a_k = _mm_nt(lhs, _stack_masked(k_t[ch]))
        a_b = _mm_nt(lhs, _stack_masked(b_t[ch]))
        a_v[ch] = jnp.concatenate([jnp.where(strict, a_k[:CHUNK], 0.0), jnp.where(incl, a_k[CHUNK:], 0.0)], axis=0)
        a_rb[ch] = jnp.where(incl, a_b[CHUNK:], 0.0)
        a_kb[ch] = jnp.where(strict, a_b[:CHUNK], 0.0)
    for ch in chains:
        pp[ch], tt[ch] = _inverse_init(a_kb[ch], eye)
    for _ in range(INVERSE_DOUBLINGS):
        for ch in chains:
            pp[ch], tt[ch] = _inverse_double(pp[ch], tt[ch])
    for ch in chains:
        tt[ch] = _inverse_last(pp[ch], tt[ch])

    w_mat, u0, y0 = {}, {}, {}
    for ch in chains:
        av = _mm(a_v[ch], _stack_masked(vv[ch]))
        y0[ch] = av[CHUNK:]
        u0[ch] = av[:CHUNK]
        w_mat[ch] = _mm(tt[ch], _stack_masked(kk_t[ch]))
    for ch in chains:
        u0[ch] = _mm(tt[ch], _stack_masked(u0[ch]))

    state = {(bi, blk): s_ref[bi * nblk + blk] for (bi, blk) in lanes}
    for c in order:
        ws = {ln: _mm_nt(jnp.concatenate([_bf(w_mat[(*ln, c)]), r_t[(*ln, c)]], axis=0), state[ln]) for ln in lanes}
        u = {ln: ws[ln][:CHUNK] + u0[(*ln, c)] for ln in lanes}
        ds = {ln: _mm_tn(jnp.concatenate([vv[(*ln, c)], _bf(-u[ln])], axis=0),
                         jnp.concatenate([k_t[(*ln, c)], b_t[(*ln, c)]], axis=0)) for ln in lanes}
        for (bi, blk) in lanes:
            ln = (bi, blk)
            state[ln] = (state[ln] + jnp.where(bd_mask, ds[ln], 0.0)) * gc_ref[bi, c, :, blk * gw:(blk + 1) * gw]
            y = ws[ln][CHUNK:] + y0[(bi, blk, c)] - _mm(a_rb[(bi, blk, c)], _stack_masked(u[ln]))
            where = (bi, slice(c * CHUNK, (c + 1) * CHUNK), slice(blk * gw, (blk + 1) * gw))
            y_ref[where] = y if acc_ref is None else y + acc_ref[where]
    for (bi, blk) in lanes:
        s_ref[bi * nblk + blk] = state[(bi, blk)]


def _rw_scan(v, r_t, k_t, b_t, kk_t, gc, batch, tt, reverse, acc=None):
    t = v.shape[0]
    seq = t // batch
    nt = seq // tt
    cpt = tt // CHUNK
    rowi = (lambda c: nt - 1 - c) if reverse else (lambda c: c)
    spec = pl.BlockSpec((batch, tt, RW_WIDTH), lambda c: (0, rowi(c), 0))
    seqs = lambda a: a.reshape(batch, seq, RW_WIDTH)
    extra = [] if acc is None else [seqs(acc)]
    y = pl.pallas_call(
        functools.partial(_rw_scan_kernel, reverse),
        grid=(nt,),
        in_specs=([spec] * 5 + [pl.BlockSpec((batch, cpt, 1, RW_WIDTH), lambda c: (0, rowi(c), 0, 0))]
                  + [spec] * len(extra)),
        out_specs=spec,
        out_shape=jax.ShapeDtypeStruct((batch, seq, RW_WIDTH), F32),
        scratch_shapes=[pltpu.VMEM((batch * RW_WIDTH // RW_GROUP_W, RW_GROUP_W, RW_GROUP_W), F32)],
        compiler_params=_cparams(("arbitrary",)),
        name="rw_scan_bwd" if reverse else "rw_scan_fwd",
    )(seqs(v), seqs(r_t), seqs(k_t), seqs(b_t), seqs(kk_t), gc.reshape(batch, seq // CHUNK, 1, RW_WIDTH), *extra)
    return y.reshape(t, RW_WIDTH)


def _rw_post_math(y, bonus, g, gn_w, gn_b, bd):
    inv_n = 1.0 / RW_HEAD_DIM
    mean = _seg_sum(y, bd) * inv_n
    yc = y - mean
    var = _seg_sum(yc * yc, bd) * inv_n
    yn = yc * lax.rsqrt(var + RW_GN_EPS) * gn_w + gn_b
    return _bf((yn + bonus) * g)


def _gdn_front_kernel(tiles_per_seq, u_ref, uprev_ref, unext_ref, w_ref, cw_ref, alog_ref, dtb_ref, ones_ref,
                      q_o, k_o, v_o, gates_o):
    pos = pl.program_id(0) % tiles_per_seq
    ts = u_ref.shape[0]
    n = ts + 2 * PAD
    half = GDN_CONV // 2
    u = _rows_with_halo(u_ref, uprev_ref, unext_ref)

    def project(c0, c1):
        return _halo_tile(jnp.dot(u, w_ref[:, c0:c1], preferred_element_type=F32), pos, tiles_per_seq)

    def conv_silu(e, c0, c1):
        acc = None
        for j in range(GDN_CONV):
            shift = (half - j) % n
            xs = e if shift == 0 else pltpu.roll(e, shift, axis=0)
            term = cw_ref[j:j + 1, c0:c1] * xs[PAD:PAD + ts, :]
            acc = term if acc is None else acc + term
        return acc * _sigmoid(acc)

    ones = ones_ref[...]
    qw = GDN_QK_WIDTH
    e_q = project(0, qw)
    e_k = project(qw, 2 * qw)
    q = conv_silu(e_q, 0, qw)
    e_v0 = project(2 * qw, 3 * qw)
    q_o[...] = q * lax.rsqrt(_seg_sum(q * q, ones) + L2_EPS) * (GDN_HEAD_DIM ** -0.5)
    k = conv_silu(e_k, qw, 2 * qw)
    e_v1 = project(3 * qw, 4 * qw)
    k_o[...] = k * lax.rsqrt(_seg_sum(k * k, ones) + L2_EPS)
    v_o[:, 0:qw] = conv_silu(e_v0, 2 * qw, 3 * qw)
    gx = jnp.dot(u_ref[...], w_ref[:, GDN_CONV_CH:], preferred_element_type=F32)
    v_o[:, qw:2 * qw] = conv_silu(e_v1, 3 * qw, 4 * qw)

    lane = lax.broadcasted_iota(jnp.int32, gx.shape, 1)
    log_decay = -jnp.exp(alog_ref[...]) * _softplus(gx + dtb_ref[...])
    gates_o[...] = jnp.where(lane < 2 * GDN_V_HEADS, log_decay, _sigmoid(gx))


def _gdn_front(u, w_gdn, seq, ts, conv_w, a_log, dt_bias, ones128):
    t, d = u.shape
    nhalo = t // HALO
    rpb = ts // HALO
    params = [conv_w, a_log, dt_bias, ones128]
    widths = (GDN_QK_WIDTH, GDN_QK_WIDTH, GDN_V_WIDTH, LANES)
    return pl.pallas_call(
        functools.partial(_gdn_front_kernel, seq // ts),
        grid=(t // ts,),
        in_specs=[pl.BlockSpec((ts, d), lambda i: (i, 0)),
                  pl.BlockSpec((HALO, d), lambda i: (jnp.maximum(i * rpb - 1, 0), 0)),
                  pl.BlockSpec((HALO, d), lambda i: (jnp.minimum((i + 1) * rpb, nhalo - 1), 0)),
                  pl.BlockSpec(w_gdn.shape, lambda i: (0, 0), pipeline_mode=pl.Buffered(1))]
        + [pl.BlockSpec(a.shape, lambda i: (0, 0)) for a in params],
        out_specs=[pl.BlockSpec((ts, wd), lambda i: (i, 0)) for wd in widths],
        out_shape=[jax.ShapeDtypeStruct((t, wd), F32) for wd in widths],
        compiler_params=_cparams(("parallel",)),
        name="gdn_front",
    )(u, u, u, w_gdn, *params)


def _gdn_scan_kernel(reverse, g_off, b_off, q_ref, k_ref, v_ref, gates_ref, *rest):
    acc_ref = rest[0] if len(rest) == 3 else None
    o_ref, s_ref = rest[-2:]

    @pl.when(pl.program_id(0) == 0)
    def _():
        s_ref[...] = jnp.zeros_like(s_ref)

    nb = q_ref.shape[0]
    nch = q_ref.shape[1] // CHUNK
    strict, incl, _ = _side_masks(reverse, 2)
    eye4 = _side_masks(reverse, 4)[2]
    tri = _tri_incl(reverse)
    s_i = lax.broadcasted_iota(jnp.int32, (CHUNK, 2 * CHUNK), 0)
    t_i = lax.broadcasted_iota(jnp.int32, (CHUNK, 2 * CHUNK), 1) % CHUNK
    tri_t2 = jnp.where((s_i >= t_i) if reverse else (s_i <= t_i), 1.0, 0.0).astype(BF16)
    first = lax.broadcasted_iota(jnp.int32, (CHUNK, 2 * CHUNK), 1) < CHUNK
    zero = jnp.zeros((CHUNK, LANES), F32)
    zero_s = jnp.zeros((LANES, LANES), F32)
    last = 0 if reverse else CHUNK - 1
    order = list(range(nch - 1, -1, -1) if reverse else range(nch))
    slots = [(bi, c) for c in order for bi in range(nb)]
    chains = [(bi, j, c) for (bi, c) in slots for j in range(GDN_QK_HEADS)]

    def rows(c):
        return slice(c * CHUNK, (c + 1) * CHUNK)

    gates, ccol, crow = {}, {}, {}
    for sl in slots:
        gates[sl] = gates_ref[sl[0], rows(sl[1]), :]
        ccol[sl] = _mm01_left(tri, gates[sl])
        crow[sl] = _mm01_tn(gates[sl], tri_t2)

    q, k, qk_pair, a_pair = {}, {}, {}, {}
    for ch in chains:
        bi, j, c = ch
        sl = (bi, c)
        i0, i1 = g_off + 2 * j, g_off + 2 * j + 1
        q[ch] = q_ref[bi, rows(c), j * LANES:(j + 1) * LANES]
        k[ch] = k_ref[bi, rows(c), j * LANES:(j + 1) * LANES]
        diff = (jnp.where(first, ccol[sl][:, i0:i0 + 1], ccol[sl][:, i1:i1 + 1])
                - jnp.where(first, crow[sl][i0:i0 + 1, :], crow[sl][i1:i1 + 1, :]))
        decay = jnp.where(incl, jnp.exp(jnp.minimum(diff, 0.0)), 0.0)
        beta_pair = jnp.where(first, gates[sl][:, b_off + 2 * j:b_off + 2 * j + 1],
                              gates[sl][:, b_off + 2 * j + 1:b_off + 2 * j + 2])
        kdup = jnp.concatenate([k[ch], k[ch]], axis=0)
        kq = _mm_nt(jnp.concatenate([k[ch], q[ch]], axis=0), kdup)
        qk_pair[ch] = kq[CHUNK:] * decay
        a_pair[ch] = jnp.where(strict, beta_pair * kq[:CHUNK] * decay, 0.0)

    quads = [(bi, jj, c) for (bi, c) in slots for jj in range(GDN_QK_HEADS // 2)]
    pp, tq = {}, {}
    for (bi, jj, c) in quads:
        pp[(bi, jj, c)], tq[(bi, jj, c)] = _inverse_init(
            jnp.concatenate([a_pair[(bi, 2 * jj, c)], a_pair[(bi, 2 * jj + 1, c)]], axis=1), eye4)
    for _ in range(INVERSE_DOUBLINGS):
        for qd in quads:
            pp[qd], tq[qd] = _inverse_double(pp[qd], tq[qd])
    tt = {}
    for (bi, jj, c) in quads:
        t4 = _inverse_last(pp[(bi, jj, c)], tq[(bi, jj, c)])
        tt[(bi, 2 * jj, c)] = t4[:, :LANES]
        tt[(bi, 2 * jj + 1, c)] = t4[:, LANES:]

    sol, wq = {}, {}
    for ch in chains:
        bi, j, c = ch
        parts = []
        for e in range(2):
            h = 2 * j + e
            beta = gates[(bi, c)][:, b_off + h:b_off + h + 1]
            eg = jnp.exp(ccol[(bi, c)][:, g_off + h:g_off + h + 1])
            v_e = v_ref[bi, rows(c), h * LANES:(h + 1) * LANES]
            blocks = [zero] * 4
            blocks[2 * e] = v_e * beta
            blocks[2 * e + 1] = k[ch] * (beta * eg)
            parts.append(jnp.concatenate(blocks, axis=1))
            wq[(bi, j, c, e)] = q[ch] * eg
        sol[ch] = _mm(tt[ch], jnp.concatenate(parts, axis=0))

    pairs = [(bi, j) for bi in range(nb) for j in range(GDN_QK_HEADS)]
    heads = [(bi, j, e) for (bi, j) in pairs for e in range(2)]
    state = {(bi, j, e): s_ref[(bi * GDN_QK_HEADS + j) * 2 + e] for (bi, j, e) in heads}
    for c in order:
        ws = {}
        for (bi, j) in pairs:
            so = sol[(bi, j, c)]
            lhs = jnp.concatenate([jnp.concatenate([so[:, LANES:2 * LANES], so[:, 3 * LANES:]], axis=1),
                                   jnp.concatenate([wq[(bi, j, c, 0)], wq[(bi, j, c, 1)]], axis=1)], axis=0)
            s_bd = jnp.concatenate([jnp.concatenate([state[(bi, j, 0)], zero_s], axis=1),
                                    jnp.concatenate([zero_s, state[(bi, j, 1)]], axis=1)], axis=0)
            ws[(bi, j)] = _mm(lhs, s_bd)
        vn = {(bi, j, e): (sol[(bi, j, c)][:, 2 * e * LANES:(2 * e + 1) * LANES]
                           - ws[(bi, j)][:CHUNK, e * LANES:(e + 1) * LANES]) for (bi, j, e) in heads}
        for (bi, j, e) in heads:
            cc = ccol[(bi, c)][:, g_off + 2 * j + e:g_off + 2 * j + e + 1]
            tot = cc[last:last + 1, :]
            kg = k[(bi, j, c)] * jnp.exp(tot - cc)
            state[(bi, j, e)] = state[(bi, j, e)] * jnp.exp(tot) + _mm_tn(kg, vn[(bi, j, e)])
        for (bi, j) in pairs:
            vn_bd = jnp.concatenate([jnp.concatenate([vn[(bi, j, 0)], zero], axis=1),
                                     jnp.concatenate([zero, vn[(bi, j, 1)]], axis=1)], axis=0)
            o_intra = _mm(qk_pair[(bi, j, c)], vn_bd)
            where = (bi, rows(c), slice(2 * j * LANES, (2 * j + 2) * LANES))
            o = ws[(bi, j)][CHUNK:] + o_intra
            o_ref[where] = o if acc_ref is None else o + acc_ref[where]
    for (bi, j, e) in heads:
        s_ref[(bi * GDN_QK_HEADS + j) * 2 + e] = state[(bi, j, e)]


def _gdn_scan(q, k, v, gates, batch, tt, reverse, acc=None):
    t = q.shape[0]
    seq = t // batch
    nt = seq // tt
    rowi = (lambda c: nt - 1 - c) if reverse else (lambda c: c)
    g_off = GDN_V_HEADS if reverse else 0
    b_off = 2 * GDN_V_HEADS + g_off
    spec = lambda width: pl.BlockSpec((batch, tt, width), lambda c: (0, rowi(c), 0))
    seqs = lambda a: a.reshape(batch, seq, a.shape[-1])
    extra = [] if acc is None else [seqs(acc)]
    o = pl.pallas_call(
        functools.partial(_gdn_scan_kernel, reverse, g_off, b_off),
        grid=(nt,),
        in_specs=[spec(GDN_QK_WIDTH), spec(GDN_QK_WIDTH), spec(GDN_V_WIDTH), spec(LANES)]
        + [spec(GDN_V_WIDTH)] * len(extra),
        out_specs=spec(GDN_V_WIDTH),
        out_shape=jax.ShapeDtypeStruct((batch, seq, GDN_V_WIDTH), F32),
        scratch_shapes=[pltpu.VMEM((batch * GDN_V_HEADS, LANES, LANES), F32)],
        compiler_params=_cparams(("arbitrary",)),
        name="gdn_scan_bwd" if reverse else "gdn_scan_fwd",
    )(seqs(q), seqs(k), seqs(v), seqs(gates), *extra)
    return o.reshape(t, GDN_V_WIDTH)


def _gdn_post_math(o, z, norm_w, ones):
    ms = _seg_sum(o * o, ones) * (1.0 / GDN_HEAD_DIM)
    return _bf(o * lax.rsqrt(ms + GDN_NORM_EPS) * norm_w * (z * _sigmoid(z)))


MERGE_SUB = 128


def _mix_out_kernel(y_ref, bonus_ref, g_ref, gnw_ref, gnb_ref, bd_ref, o_ref, z_ref, nw_ref, ones_ref,
                    gr_ref, gg_ref, pa_ref, pb_ref, x_ref, wo_ref, gpost_ref, gnext_ref, h_ref, u_ref):
    parts = []
    for s in range(y_ref.shape[0] // MERGE_SUB):
        rows = slice(s * MERGE_SUB, (s + 1) * MERGE_SUB)
        y_rw = _rw_post_math(y_ref[rows, :], bonus_ref[rows, :], g_ref[rows, :],
                             gnw_ref[...], gnb_ref[...], bd_ref[...])
        y_gdn = _gdn_post_math(o_ref[rows, :], z_ref[rows, :], nw_ref[...], ones_ref[...])
        a = jnp.dot(y_rw, pa_ref[...], preferred_element_type=F32)
        b = jnp.dot(y_gdn, pb_ref[...], preferred_element_type=F32)
        parts.append(_bf(_sigmoid(gr_ref[rows, :]) * a + _sigmoid(gg_ref[rows, :]) * b))
    m = jnp.concatenate(parts, axis=0)
    yo = jnp.dot(m, wo_ref[...], preferred_element_type=F32)
    ms = jnp.mean(yo * yo, axis=-1, keepdims=True)
    h = x_ref[...] + yo * lax.rsqrt(ms + RMS_EPS) * gpost_ref[...]
    h_ref[...] = h
    ms2 = jnp.mean(h * h, axis=-1, keepdims=True)
    u_ref[...] = _bf(h * lax.rsqrt(ms2 + RMS_EPS) * gnext_ref[...])


def _mix_out(y, bonus, g, gn_w, gn_b, bd64, o, p, norm_w, ones128, w_a, w_b, x, w_out, gain, gain_next, tm):
    t, d = x.shape
    wide = pl.BlockSpec((tm, RW_WIDTH), lambda i: (i, 0))
    row = lambda width: pl.BlockSpec((1, width), lambda i: (0, 0))
    const = lambda a: pl.BlockSpec(a.shape, lambda i: (0, 0))
    weight = lambda a: pl.BlockSpec(a.shape, lambda i: (0, 0), pipeline_mode=pl.Buffered(1))
    pcol = lambda width, off: pl.BlockSpec((tm, width), lambda i: (i, off // width))
    tile = pl.BlockSpec((tm, d), lambda i: (i, 0))
    return pl.pallas_call(
        _mix_out_kernel,
        grid=(t // tm,),
        in_specs=[wide, wide, wide, row(RW_WIDTH), row(RW_WIDTH), const(bd64),
                  wide, pcol(GDN_V_WIDTH, P_GZ), row(GDN_V_WIDTH), const(ones128),
                  pcol(d, P_MG_RW), pcol(d, P_MG_GDN), weight(w_a), weight(w_b),
                  tile, weight(w_out), row(d), row(d)],
        out_specs=[tile, tile],
        out_shape=[jax.ShapeDtypeStruct((t, d), F32), jax.ShapeDtypeStruct((t, d), BF16)],
        compiler_params=pltpu.CompilerParams(dimension_semantics=("parallel",), vmem_limit_bytes=FRONT_VMEM_LIMIT),
        name="mix_out",
    )(y, bonus, g, gn_w, gn_b, bd64, o, p, norm_w, ones128, p, p, w_a, w_b, x, w_out, gain, gain_next)


def _ffn_kernel(h_ref, u_ref, wg_ref, wu_ref, wd_ref, gpost_ref, o_ref):
    j = pl.program_id(1)

    @pl.when(j == 0)
    def _():
        o_ref[...] = jnp.zeros_like(o_ref)

    u = u_ref[...]
    gate = jnp.dot(u, wg_ref[...], preferred_element_type=F32)
    up = jnp.dot(u, wu_ref[...], preferred_element_type=F32)
    f = _bf(gate * _sigmoid(gate) * up)
    o_ref[...] += jnp.dot(f, wd_ref[...], preferred_element_type=F32)

    @pl.when(j == pl.num_programs(1) - 1)
    def _():
        y = o_ref[...]
        ms = jnp.mean(y * y, axis=-1, keepdims=True)
        o_ref[...] = h_ref[...] + y * lax.rsqrt(ms + RMS_EPS) * gpost_ref[...]


def _ffn(h, u, wg, wu, wd, gpost, tm, tf):
    t, d = h.shape
    f = wg.shape[1]
    assert t % tm == 0 and f % tf == 0, (t, tm, f, tf)
    return pl.pallas_call(
        _ffn_kernel,
        grid=(t // tm, f // tf),
        in_specs=[pl.BlockSpec((tm, d), lambda i, j: (i, 0)),
                  pl.BlockSpec((tm, d), lambda i, j: (i, 0)),
                  pl.BlockSpec((d, tf), lambda i, j: (0, j)),
                  pl.BlockSpec((d, tf), lambda i, j: (0, j)),
                  pl.BlockSpec((tf, d), lambda i, j: (j, 0)),
                  pl.BlockSpec((1, d), lambda i, j: (0, 0))],
        out_specs=pl.BlockSpec((tm, d), lambda i, j: (i, 0)),
        out_shape=jax.ShapeDtypeStruct((t, d), F32),
        compiler_params=_cparams(("parallel", "arbitrary")),
        name="ffn",
    )(h, u, wg, wu, wd, gpost)


def _pack_w_in(w_in):
    d = w_in.shape[0]
    rw = w_in[:, :RW_IN]
    gdn = w_in[:, RW_IN:RW_IN + GDN_IN]
    gates = w_in[:, RW_IN + GDN_IN:]
    z = lambda n: jnp.zeros((d, n), w_in.dtype)
    cat = lambda cols: _bf(jnp.concatenate(cols, axis=1))
    w_rw = cat([rw, z(RW_BLOCK - RW_IN)])
    w_gdn = cat([gdn[:, :GDN_CONV_CH], gdn[:, GDN_CONV_CH + GDN_V_WIDTH:], z(LANES - 4 * GDN_V_HEADS)])
    w_rest = cat([gates, gdn[:, GDN_CONV_CH:GDN_CONV_CH + GDN_V_WIDTH]])
    return w_rw, w_gdn, w_rest


def _block_diag2(a, b):
    z = jnp.zeros_like(a)
    return jnp.concatenate([jnp.concatenate([a, z], axis=1), jnp.concatenate([z, b], axis=1)], axis=0)


def _group_ones(group):
    i = jnp.arange(MXU_TILE)
    return (i[:, None] // group == i[None, :] // group).astype(BF16)


def _forward(x, norm_pre_mix, w_in, rw_shift_mu, rw_w0_f, rw_w2_f, rw_w0_b, rw_w2_b, rw_a0_f, rw_a2_f, rw_a0_b,
             rw_a2_b, rw_g2, rw_k_k, rw_k_a, rw_r_k, rw_gn_w, rw_gn_b, gdn_conv_w, gdn_a_log_f, gdn_dt_bias_f,
             gdn_a_log_b, gdn_dt_bias_b, gdn_norm_w, w_branch_rw, w_branch_gdn, w_out, norm_post_mix, norm_pre_ffn,
             w_ffn_gate, w_ffn_up, w_ffn_down, norm_post_ffn, *, tiles):
    batch, seq, d = x.shape
    t = batch * seq
    xf = x.reshape(t, d)
    row = lambda a: a.reshape(1, -1).astype(F32)
    bd64 = _group_ones(RW_HEAD_DIM)
    ones128 = _group_ones(GDN_HEAD_DIM)

    w_rw, w_gdn, w_rest = _pack_w_in(w_in)
    p, u_in = _norm_matmul(xf, row(norm_pre_mix), w_rest, tiles["in_tm"], tiles["in_tn"])

    mu = jnp.pad(row(rw_shift_mu), ((0, 0), (0, RW_BLOCK - RW_IN)))
    w0 = jnp.concatenate([row(rw_w0_f), row(rw_w0_b)], axis=1)
    a0 = jnp.concatenate([row(rw_a0_f), row(rw_a0_b)], axis=1)
    w2 = _bf(_block_diag2(rw_w2_f, rw_w2_b))
    a2 = _bf(_block_diag2(rw_a2_f, rw_a2_b))
    g2 = _bf(jnp.pad(rw_g2, ((0, 2 * LANES - RW_GATE_LORA), (0, 0))))
    v, g, bonus, *scaled = _rw_front(
        u_in, w_rw, seq, tiles["prep_ts"], mu, w0, w2, a0, a2, g2, row(rw_k_k), row(rw_k_a), row(rw_r_k), bd64)
    y_f = _rw_scan(v, *scaled[:5], batch, tiles["rw_tt"], False)
    y = _rw_scan(v, *scaled[5:], batch, tiles["rw_tt"], True, acc=y_f)

    conv_w = jnp.pad(gdn_conv_w.astype(F32), ((0, SUBLANES - GDN_CONV), (0, 0)))
    nv = GDN_V_HEADS
    a_log = jnp.pad(jnp.concatenate([gdn_a_log_f, gdn_a_log_b]).reshape(1, -1), ((0, 0), (0, LANES - 2 * nv)))
    dt_bias = jnp.pad(jnp.concatenate([gdn_dt_bias_f, gdn_dt_bias_b]).reshape(1, -1), ((0, 0), (0, LANES - 2 * nv)))
    q, k, vg, gates = _gdn_front(u_in, w_gdn, seq, tiles["prep_ts"], conv_w, a_log.astype(F32),
                                 dt_bias.astype(F32), ones128)
    o_f = _gdn_scan(q, k, vg, gates, batch, tiles["gdn_tt"], False)
    o = _gdn_scan(q, k, vg, gates, batch, tiles["gdn_tt"], True, acc=o_f)
    nw = jnp.tile(row(gdn_norm_w), (1, GDN_V_HEADS))

    h1, u_ffn = _mix_out(y, bonus, g, row(rw_gn_w), row(rw_gn_b), bd64, o, p, nw, ones128,
                         _bf(w_branch_rw), _bf(w_branch_gdn), xf, _bf(w_out), row(norm_post_mix),
                         row(norm_pre_ffn), tiles["mix_tm"])
    h2 = _ffn(h1, u_ffn, _bf(w_ffn_gate), _bf(w_ffn_up), _bf(w_ffn_down), row(norm_post_ffn),
              tiles["ffn_tm"], tiles["ffn_tf"])
    return h2.reshape(batch, seq, d)


def _tiles(seq):
    pick = lambda want: min(want, seq)
    return dict(in_tm=pick(1024), in_tn=1024, prep_ts=pick(256), rw_tt=pick(256), gdn_tt=pick(256),
                mix_tm=pick(256), ffn_tm=pick(512), ffn_tf=512)


def kernel(x, norm_pre_mix, w_in, rw_shift_mu, rw_w0_f, rw_w2_f, rw_w0_b, rw_w2_b, rw_a0_f, rw_a2_f, rw_a0_b, rw_a2_b, rw_g2, rw_k_k, rw_k_a, rw_r_k, rw_gn_w, rw_gn_b, gdn_conv_w, gdn_a_log_f, gdn_dt_bias_f, gdn_a_log_b, gdn_dt_bias_b, gdn_norm_w, w_branch_rw, w_branch_gdn, w_out, norm_post_mix, norm_pre_ffn, w_ffn_gate, w_ffn_up, w_ffn_down, norm_post_ffn):
    args = [a[0] for a in (norm_pre_mix, w_in, rw_shift_mu, rw_w0_f, rw_w2_f, rw_w0_b, rw_w2_b, rw_a0_f, rw_a2_f,
                           rw_a0_b, rw_a2_b, rw_g2, rw_k_k, rw_k_a, rw_r_k, rw_gn_w, rw_gn_b, gdn_conv_w,
                           gdn_a_log_f, gdn_dt_bias_f, gdn_a_log_b, gdn_dt_bias_b, gdn_norm_w, w_branch_rw,
                           w_branch_gdn, w_out, norm_post_mix, norm_pre_ffn, w_ffn_gate, w_ffn_up, w_ffn_down,
                           norm_post_ffn)]
    assert norm_pre_mix.shape[0] == 1, "one layer"
    return _forward(x, *args, tiles=_tiles(x.shape[1]))
```
